```python
import jax, jax.numpy as jnp
from jax import lax
import numpy as np

D_MODEL = 1024
BATCH = 4
SEQ = 8192
DEPTH = 1

CHUNK = 64
EPS = 1e-6
ROPE_THETA = 10000.0
N_BRANCH = 3
D_BRANCH = 1024
MLSTM_HEADS = 4
MLSTM_HD = 256
D_MLSTM = MLSTM_HEADS * MLSTM_HD
CONV_W = 4
MLSTM_F_BIAS = 3.0
DSA_HEADS = 8
DSA_HD = 128
D_DSA = DSA_HEADS * DSA_HD
IDX_HEADS = 8
IDX_HD = 64
DSA_TOPK = 256
Q_BLOCK = 128
N_MEM = 256
MEM_HEADS = 4
MEM_HD = 256
D_MEM = MEM_HEADS * MEM_HD
SPLITS = (2 * D_MLSTM, D_MLSTM, 2 * MLSTM_HEADS, D_MLSTM, D_MLSTM,
          D_DSA, D_DSA, D_DSA, D_DSA, IDX_HEADS * IDX_HD, IDX_HD, IDX_HEADS,
          D_MEM, D_MEM, N_BRANCH * D_MODEL)
D_IN = (2 * D_MLSTM + D_MLSTM + 2 * MLSTM_HEADS + D_MLSTM + D_MLSTM
        + 4 * D_DSA + IDX_HEADS * IDX_HD + IDX_HD + IDX_HEADS
        + 2 * D_MEM + N_BRANCH * D_MODEL)

kernel_name = 'hybrid_mlstm_dsa_memxattn_block'


def rms_norm(x, g):
    x32 = x.astype(jnp.float32)
    y = x32 * lax.rsqrt(jnp.mean(x32 * x32, axis=-1, keepdims=True) + EPS)
    return (y * g.astype(jnp.float32)).astype(x.dtype)


def rotary(x, positions):
    d = x.shape[-1]
    half = d // 2
    inv_freq = jnp.power(ROPE_THETA, -jnp.arange(half, dtype=jnp.float32) / half)
    ang = positions.astype(jnp.float32)[..., None] * inv_freq
    cos = jnp.cos(ang)[:, :, None, :]
    sin = jnp.sin(ang)[:, :, None, :]
    x32 = x.astype(jnp.float32)
    x1, x2 = x32[..., :half], x32[..., half:]
    return jnp.concatenate([x1 * cos - x2 * sin, x2 * cos + x1 * sin], axis=-1).astype(x.dtype)


def causal_conv(u, w, b):
    c = u.shape[-1]
    out = lax.conv_general_dilated(u, w[:, None, :].astype(u.dtype), window_strides=(1,),
                                   padding=[(CONV_W - 1, 0)],
                                   dimension_numbers=('NWC', 'WIO', 'NWC'),
                                   feature_group_count=c)
    return out + b.astype(u.dtype)


def mlstm_scan(q, k, v, i_pre, log_f):
    B, S, H, dh = q.shape
    nc = S // CHUNK
    to_c4 = lambda a: a.reshape(B, nc, CHUNK, H, dh).transpose(1, 0, 3, 2, 4)
    to_c3 = lambda a: a.reshape(B, nc, CHUNK, H).transpose(1, 0, 3, 2)
    tril = jnp.tril(jnp.ones((CHUNK, CHUNK), dtype=bool))

    def step(carry, xs):
        C, n, m = carry
        qc, kc, vc, ic, fc = xs
        b = jnp.cumsum(fc, axis=-1)
        d_log = b[..., :, None] - b[..., None, :] + ic[..., None, :]
        d_log = jnp.where(tril, d_log, -jnp.inf)
        inter = b + m[..., None]
        m_t = jnp.maximum(inter, jnp.max(d_log, axis=-1))
        w_intra = jnp.exp(d_log - m_t[..., None])
        w_inter = jnp.exp(inter - m_t)
        a = jnp.einsum('bhtd,bhsd->bhts', qc, kc) * w_intra
        num = (w_inter[..., None] * jnp.einsum('bhtd,bhde->bhte', qc, C)
               + jnp.einsum('bhts,bhse->bhte', a, vc))
        den = w_inter * jnp.einsum('bhtd,bhd->bht', qc, n) + jnp.sum(a, axis=-1)
        h = num / jnp.maximum(jnp.abs(den), jnp.exp(-m_t))[..., None]
        b_last = b[..., -1]
        g = b_last[..., None] - b + ic
        m_new = jnp.maximum(b_last + m, jnp.max(g, axis=-1))
        decay = jnp.exp(b_last + m - m_new)
        wk = jnp.exp(g - m_new[..., None])
        C = decay[..., None, None] * C + jnp.einsum('bhs,bhsd,bhse->bhde', wk, kc, vc)
        n = decay[..., None] * n + jnp.einsum('bhs,bhsd->bhd', wk, kc)
        return (C, n, m_new), h

    init = (jnp.zeros((B, H, dh, dh), jnp.float32), jnp.zeros((B, H, dh), jnp.float32),
            jnp.zeros((B, H), jnp.float32))
    _, hs = lax.scan(step, init, (to_c4(q), to_c4(k), to_c4(v), to_c3(i_pre), to_c3(log_f)))
    return hs.transpose(1, 0, 3, 2, 4).reshape(B, S, H, dh)


def dsa_attention(q, k, v, qi, ki, wi):
    B, S, H, dh = q.shape
    top_k = min(DSA_TOPK, S // 4)
    nb = S // Q_BLOCK
    blocks = lambda a: a.reshape((B, nb, Q_BLOCK) + a.shape[2:]).swapaxes(0, 1)
    starts = jnp.arange(nb, dtype=jnp.int32) * Q_BLOCK
    key_pos = jnp.arange(S, dtype=jnp.int32)
    gather = jax.vmap(lambda kb, ib: kb[ib])

    def one_block(args):
        qb, qib, wib, t0 = args
        t = t0 + jnp.arange(Q_BLOCK, dtype=jnp.int32)
        visible_end = (t // CHUNK + 1) * CHUNK
        logits = jnp.einsum('bqhd,bsd->bqhs', qib, ki).astype(jnp.float32) * (IDX_HD ** -0.5)
        score = jnp.einsum('bqhs,bqh->bqs', jax.nn.relu(logits), wib.astype(jnp.float32))
        admissible = key_pos[None, :] < visible_end[:, None]
        score = jnp.where(admissible[None], score, -jnp.inf)
        _, idx = lax.top_k(score, top_k)
        valid = idx < visible_end[None, :, None]
        k_sel = gather(k, idx)
        v_sel = gather(v, idx)
        s = jnp.einsum('bqhd,bqkhd->bhqk', qb, k_sel).astype(jnp.float32) * (dh ** -0.5)
        s = jnp.where(valid[:, None], s, -jnp.inf)
        p = jax.nn.softmax(s, axis=-1)
        return jnp.einsum('bhqk,bqkhd->bqhd', p.astype(v.dtype), v_sel)

    out = lax.map(one_block, (blocks(q), blocks(qi), blocks(wi), starts))
    return out.swapaxes(0, 1).reshape(B, S, H, dh)


def memory_attention(q, k, v):
    s = jnp.einsum('bshd,bmhd->bhsm', q, k).astype(jnp.float32) * (q.shape[-1] ** -0.5)
    p = jax.nn.softmax(s, axis=-1)
    return jnp.einsum('bhsm,bmhd->bshd', p.astype(v.dtype), v)


def setup_inputs(seed: int = 0) -> dict:
    key = jax.random.key(seed)
    ks = jax.random.split(key, 18)
    f32 = jnp.float32
    nrm = lambda k, shape, scale: jax.random.normal(k, shape, f32) * scale
    gain = lambda k, shape: 1.0 + 0.02 * jax.random.normal(k, shape, f32)
    x = jax.random.normal(ks[0], (BATCH, SEQ, D_MODEL), f32)
    mem = jax.random.normal(ks[1], (BATCH, N_MEM, D_MODEL), f32)
    offsets = jax.random.randint(ks[2], (BATCH, 1), 0, 1024, dtype=jnp.int32)
    positions = offsets + jnp.arange(SEQ, dtype=jnp.int32)[None, :]
    gate_off = jnp.array([0.0, MLSTM_F_BIAS], f32)[None, :, None]
    return {
        'x': x,
        'mem': mem,
        'positions': positions,
        'norm_g': gain(ks[3], (DEPTH, D_MODEL)),
        'mem_norm_g': gain(ks[4], (DEPTH, D_MODEL)),
        'w_in': nrm(ks[5], (DEPTH, D_MODEL, D_IN), D_MODEL ** -0.5),
        'conv_w': nrm(ks[6], (DEPTH, CONV_W, 2 * D_MLSTM), CONV_W ** -0.5),
        'conv_b': nrm(ks[7], (DEPTH, 2 * D_MLSTM), 0.01),
        'mlstm_gate_b': gate_off + nrm(ks[8], (DEPTH, 2, MLSTM_HEADS), 0.1),
        'mlstm_norm_g': gain(ks[9], (DEPTH, MLSTM_HEADS, MLSTM_HD)),
        'dsa_q_norm_g': gain(ks[10], (DEPTH, DSA_HD)),
        'dsa_k_norm_g': gain(ks[11], (DEPTH, DSA_HD)),
        'mem_q_norm_g': gain(ks[12], (DEPTH, MEM_HD)),
        'mem_k_norm_g': gain(ks[13], (DEPTH, MEM_HD)),
        'w_mem_kv': nrm(ks[14], (DEPTH, D_MODEL, 2 * D_MEM), D_MODEL ** -0.5),
        'w_branch': nrm(ks[15], (DEPTH, N_BRANCH, D_BRANCH, D_MODEL), D_BRANCH ** -0.5),
        'w_out': nrm(ks[16], (DEPTH, D_MODEL, D_MODEL), D_MODEL ** -0.5),
    }


def reference(x, mem, positions, norm_g, mem_norm_g, w_in, conv_w, conv_b, mlstm_gate_b,
              mlstm_norm_g, dsa_q_norm_g, dsa_k_norm_g, mem_q_norm_g, mem_k_norm_g,
              w_mem_kv, w_branch, w_out):
    B, S, _ = x.shape
    dt = x.dtype
    cuts = np.cumsum(np.array(SPLITS))[:-1].tolist()
    for l in range(DEPTH):
        h = rms_norm(x, norm_g[l])
        proj = h @ w_in[l]
        (qk_a, v_a, if_a, o_a, z_a, q_b, k_b, v_b, z_b, qi, ki, wi,
         q_m, z_m, gate) = jnp.split(proj, cuts, axis=-1)

        qk_a = jax.nn.silu(causal_conv(qk_a, conv_w[l], conv_b[l])).astype(jnp.float32)
        qa = qk_a[..., :D_MLSTM].reshape(B, S, MLSTM_HEADS, MLSTM_HD)
        ka = qk_a[..., D_MLSTM:].reshape(B, S, MLSTM_HEADS, MLSTM_HD) * (MLSTM_HD ** -0.5)
        va = v_a.astype(jnp.float32).reshape(B, S, MLSTM_HEADS, MLSTM_HD)
        if_a = if_a.astype(jnp.float32)
        i_pre = if_a[..., :MLSTM_HEADS] + mlstm_gate_b[l, 0].astype(jnp.float32)
        log_f = jax.nn.log_sigmoid(if_a[..., MLSTM_HEADS:] + mlstm_gate_b[l, 1].astype(jnp.float32))
        ha = mlstm_scan(qa, ka, va, i_pre, log_f)
        ha = rms_norm(ha, mlstm_norm_g[l]).reshape(B, S, D_MLSTM).astype(dt)
        y_a = ha * jax.nn.sigmoid(o_a) * jax.nn.silu(z_a)

        qb = rotary(rms_norm(q_b.reshape(B, S, DSA_HEADS, DSA_HD), dsa_q_norm_g[l]), positions)
        kb = rotary(rms_norm(k_b.reshape(B, S, DSA_HEADS, DSA_HD), dsa_k_norm_g[l]), positions)
        vb = v_b.reshape(B, S, DSA_HEADS, DSA_HD)
        qib = rotary(qi.reshape(B, S, IDX_HEADS, IDX_HD), positions)
        kib = rotary(ki.reshape(B, S, 1, IDX_HD), positions)[:, :, 0, :]
        wib = wi * (IDX_HEADS ** -0.5)
        y_b = dsa_attention(qb, kb, vb, qib, kib, wib).reshape(B, S, D_DSA) * jax.nn.silu(z_b)

        mem_h = rms_norm(mem, mem_norm_g[l])
        kv_m = mem_h @ w_mem_kv[l]
        km = rms_norm(kv_m[..., :D_MEM].reshape(B, -1, MEM_HEADS, MEM_HD), mem_k_norm_g[l])
        vm = kv_m[..., D_MEM:].reshape(B, -1, MEM_HEADS, MEM_HD)
        qm = rms_norm(q_m.reshape(B, S, MEM_HEADS, MEM_HD), mem_q_norm_g[l])
        y_m = memory_attention(qm, km, vm).reshape(B, S, D_MEM) * jax.nn.silu(z_m)

        g = jax.nn.sigmoid(gate.astype(jnp.float32)).astype(dt).reshape(B, S, N_BRANCH, D_MODEL)
        merged = (g[:, :, 0] * (y_a @ w_branch[l, 0])
                  + g[:, :, 1] * (y_b @ w_branch[l, 1])
                  + g[:, :, 2] * (y_m @ w_branch[l, 2]))
        x = x + merged @ w_out[l]
    return x
```

```python
import functools

import numpy as np
import jax
import jax.numpy as jnp
from jax import lax
from jax.experimental import pallas as pl
from jax.experimental.pallas import tpu as pltpu

F32 = jnp.float32
BF16 = jnp.bfloat16

EPS = 1e-6
ROPE_THETA = 10000.0
D_MODEL = 1024
MLSTM_HEADS = 4
MLSTM_HD = 256
CONV_W = 4
DSA_HEADS = 8
DSA_HD = 128
IDX_HEADS = 8
IDX_HD = 64
DSA_TOPK = 256
VIS_CHUNK = 64
MEM_HEADS = 4
MEM_HD = 256
N_BRANCH = 3

LANES = 128
SCAN_CHUNK = 256
Q_TILE = 128
K_TILE = 256
ROW_TILE = 256
NEG_BIG = -1e30
INT_MIN = -2 ** 31
VMEM_LIMIT = 56 * 1024 * 1024


def _sigmoid(x):
    return 1.0 / (1.0 + jnp.exp(-x))


def _silu(x):
    return x * _sigmoid(x)


def _log_sigmoid(x):
    return jnp.minimum(x, 0.0) - jnp.log(1.0 + jnp.exp(-jnp.abs(x)))


def _dot(a, b):
    return jnp.dot(a, b, preferred_element_type=F32)


def _dot_nt(a, b):
    return lax.dot_general(a, b, (((1,), (1,)), ((), ())), preferred_element_type=F32)


def _dot_tn(a, b):
    return lax.dot_general(a, b, (((0,), (0,)), ((), ())), preferred_element_type=F32)


def _rms_rows(x, g):
    ms = jnp.mean(x * x, axis=-1, keepdims=True)
    return x * lax.rsqrt(ms + EPS) * g


def _normed_input(x_ref, g_ref):
    return _rms_rows(x_ref[...], g_ref[...]).astype(BF16)


def _rotate(x, cos, sin_signed):
    return x * cos + pltpu.roll(x, LANES // 2, 1) * sin_signed


def _proj_a_kernel(x_ref, g_ref, w_ref, wif_ref, qk_ref, v_ref, og_ref, if_ref):
    d = MLSTM_HEADS * MLSTM_HD
    h = _normed_input(x_ref, g_ref)
    qk_ref[...] = _dot(h, w_ref[:, 0:2 * d])
    v_ref[...] = _dot(h, w_ref[:, 2 * d:3 * d]).astype(BF16)
    o = _dot(h, w_ref[:, 3 * d:4 * d])
    z = _dot(h, w_ref[:, 4 * d:5 * d])
    og_ref[...] = _sigmoid(o) * _silu(z)
    if_ref[...] = _dot(h, wif_ref[...])


def _proj_a(x2, g, w_a, w_if):
    t = x2.shape[0]
    d = MLSTM_HEADS * MLSTM_HD
    tm = ROW_TILE
    row = lambda n: pl.BlockSpec((tm, n), lambda i: (i, 0))
    full = lambda a: pl.BlockSpec(a.shape, lambda i: (0, 0))
    return pl.pallas_call(
        _proj_a_kernel,
        grid=(t // tm,),
        in_specs=[row(D_MODEL), full(g), full(w_a), full(w_if)],
        out_specs=[row(2 * d), row(d), row(d), row(LANES)],
        out_shape=[jax.ShapeDtypeStruct((t, 2 * d), F32), jax.ShapeDtypeStruct((t, d), BF16),
                   jax.ShapeDtypeStruct((t, d), F32), jax.ShapeDtypeStruct((t, LANES), F32)],
        compiler_params=pltpu.CompilerParams(dimension_semantics=("arbitrary",),
                                             vmem_limit_bytes=VMEM_LIMIT),
        name="proj_a",
    )(x2, g, w_a, w_if)


def _mlstm_kernel(qk_ref, v_ref, og_ref, ifc_ref, ifr_ref, cw_ref, cb_ref, gbc_ref, gbr_ref, ng_ref,
                  y_ref, ext_ref, c_ref, n_ref, m_ref):
    L = SCAN_CHUNK
    dh = MLSTM_HD
    d = MLSTM_HEADS * dh
    halo = 8

    @pl.when(pl.program_id(1) == 0)
    def _():
        ext_ref[0:halo, :] = jnp.zeros((halo, 2 * d), F32)
        c_ref[...] = jnp.zeros_like(c_ref)
        n_ref[...] = jnp.zeros_like(n_ref)
        m_ref[...] = jnp.zeros_like(m_ref)

    ext_ref[halo:halo + L, :] = qk_ref[...]

    def conv_silu(c0):
        u = cb_ref[:, c0:c0 + dh]
        for j in range(CONV_W):
            r0 = halo - (CONV_W - 1) + j
            u = u + cw_ref[j:j + 1, c0:c0 + dh] * ext_ref[r0:r0 + L, c0:c0 + dh]
        return _silu(u)

    row_i = lax.broadcasted_iota(jnp.int32, (L, L), 0)
    col_i = lax.broadcasted_iota(jnp.int32, (L, L), 1)
    causal = row_i >= col_i
    tril = jnp.where(causal, 1.0, 0.0).astype(F32)
    triu = jnp.where(col_i >= row_i, 1.0, 0.0).astype(F32)

    ifc = ifc_ref[...] + gbc_ref[...]
    ifr = ifr_ref[...] + gbr_ref[...]
    bcol_all = jnp.dot(tril, _log_sigmoid(ifc), preferred_element_type=F32,
                       precision=lax.Precision.HIGHEST)
    brow_all = jnp.dot(_log_sigmoid(ifr), triu, preferred_element_type=F32,
                       precision=lax.Precision.HIGHEST)

    for h in range(MLSTM_HEADS):
        qf = conv_silu(h * dh)
        kf = conv_silu(d + h * dh) * (dh ** -0.5)
        q = qf.astype(BF16)
        v = v_ref[:, h * dh:(h + 1) * dh]
        icol = ifc[:, h:h + 1]
        irow = ifr[h:h + 1, :]
        bcol = bcol_all[:, MLSTM_HEADS + h:MLSTM_HEADS + h + 1]
        brow = brow_all[MLSTM_HEADS + h:MLSTM_HEADS + h + 1, :]
        m_prev = m_ref[h:h + 1, 0:1]

        d_log = jnp.where(causal, bcol - brow + irow, -jnp.inf)
        inter = bcol + m_prev
        m_t = jnp.maximum(inter, jnp.max(d_log, axis=-1, keepdims=True))
        w_intra = jnp.exp(d_log - m_t)
        w_inter = jnp.exp(inter - m_t)
        a = _dot_nt(q, kf.astype(BF16)) * w_intra
        nrow = n_ref[h]
        num = w_inter * _dot(q, c_ref[h].astype(BF16)) + _dot(a.astype(BF16), v)
        den = w_inter * jnp.sum(qf * nrow, axis=-1, keepdims=True) + jnp.sum(a, axis=-1, keepdims=True)
        hh = num / jnp.maximum(jnp.abs(den), jnp.exp(-m_t))
        hn = _rms_rows(hh, ng_ref[h])
        y_ref[:, h * dh:(h + 1) * dh] = (hn * og_ref[:, h * dh:(h + 1) * dh]).astype(BF16)

        b_last = bcol[L - 1:L, :]
        g_col = b_last - bcol + icol
        m_new = jnp.maximum(b_last + m_prev, jnp.max(g_col, axis=0, keepdims=True))
        decay = jnp.exp(b_last + m_prev - m_new)
        kw = kf * jnp.exp(g_col - m_new)
        c_ref[h] = decay * c_ref[h] + _dot_tn(kw.astype(BF16), v)
        n_ref[h] = decay * nrow + jnp.sum(kw, axis=0, keepdims=True)
        m_ref[h:h + 1, :] = jnp.broadcast_to(m_new, (1, LANES))

    ext_ref[0:halo, :] = ext_ref[L:L + halo, :]


def _mlstm(qk, v, og, ifc, ifr, conv_w, conv_b, gate_col, gate_row, norm_g, batch):
    t = qk.shape[0]
    L = SCAN_CHUNK
    d = MLSTM_HEADS * MLSTM_HD
    nc = t // batch // L
    row = lambda n: pl.BlockSpec((L, n), lambda b, c: (b * nc + c, 0))
    full = lambda a: pl.BlockSpec(a.shape, lambda b, c: (0,) * a.ndim)
    return pl.pallas_call(
        _mlstm_kernel,
        grid=(batch, nc),
        in_specs=[row(2 * d), row(d), row(d), row(LANES),
                  pl.BlockSpec((None, 8, L), lambda b, c: (b, 0, c)),
                  full(conv_w), full(conv_b), full(gate_col), full(gate_row), full(norm_g)],
        out_specs=row(d),
        out_shape=jax.ShapeDtypeStruct((t, d), BF16),
        scratch_shapes=[pltpu.VMEM((L + 8, 2 * d), F32),
                        pltpu.VMEM((MLSTM_HEADS, MLSTM_HD, MLSTM_HD), F32),
                        pltpu.VMEM((MLSTM_HEADS, 1, MLSTM_HD), F32),
                        pltpu.VMEM((8, LANES), F32)],
        compiler_params=pltpu.CompilerParams(dimension_semantics=("arbitrary", "arbitrary"),
                                             vmem_limit_bytes=VMEM_LIMIT),
        name="mlstm",
    )(qk, v, og, ifc, ifr, conv_w, conv_b, gate_col, gate_row, norm_g)


def _proj_b_kernel(x_ref, g_ref, w_ref, cosd_ref, sind_ref, cosi_ref, sini_ref, gq_ref, gk_ref,
                   q_ref, k_ref, v_ref, z_ref, qi_ref, ki_ref, wi_ref):
    d = DSA_HEADS * DSA_HD
    h = _normed_input(x_ref, g_ref)
    cosd = cosd_ref[...]
    sind = sind_ref[...]
    cosi = cosi_ref[...]
    sini = sini_ref[...]

    acc = _dot(h, w_ref[:, 0:d])
    for hd in range(DSA_HEADS):
        sl = slice(hd * DSA_HD, (hd + 1) * DSA_HD)
        xs = _rotate(_rms_rows(acc[:, sl], gq_ref[...]), cosd, sind)
        q_ref[:, sl] = (xs * (DSA_HD ** -0.5)).astype(BF16)
    acc = _dot(h, w_ref[:, d:2 * d])
    for hd in range(DSA_HEADS):
        sl = slice(hd * DSA_HD, (hd + 1) * DSA_HD)
        k_ref[:, sl] = _rotate(_rms_rows(acc[:, sl], gk_ref[...]), cosd, sind).astype(BF16)
    v_ref[...] = _dot(h, w_ref[:, 2 * d:3 * d]).astype(BF16)
    z_ref[...] = _silu(_dot(h, w_ref[:, 3 * d:4 * d]))
    acc = _dot(h, w_ref[:, 4 * d:4 * d + IDX_HEADS * LANES])
    for hd in range(IDX_HEADS):
        sl = slice(hd * LANES, (hd + 1) * LANES)
        qi_ref[:, sl] = (_rotate(acc[:, sl], cosi, sini) * (IDX_HD ** -0.5)).astype(BF16)
    c0 = 4 * d + IDX_HEADS * LANES
    ki_ref[...] = _rotate(_dot(h, w_ref[:, c0:c0 + LANES]), cosi, sini).astype(BF16)
    wi_ref[...] = _dot(h, w_ref[:, c0 + LANES:c0 + 2 * LANES]) * (IDX_HEADS ** -0.5)


def _proj_b(x2, g, w_b, cosd, sind, cosi, sini, gq, gk):
    t = x2.shape[0]
    d = DSA_HEADS * DSA_HD
    tm = ROW_TILE
    row = lambda n: pl.BlockSpec((tm, n), lambda i: (i, 0))
    full = lambda a: pl.BlockSpec(a.shape, lambda i: (0, 0))
    sds = jax.ShapeDtypeStruct
    return pl.pallas_call(
        _proj_b_kernel,
        grid=(t // tm,),
        in_specs=[row(D_MODEL), full(g), full(w_b), row(LANES), row(LANES), row(LANES), row(LANES),
                  full(gq), full(gk)],
        out_specs=[row(d), row(d), row(d), row(d), row(IDX_HEADS * LANES), row(LANES), row(LANES)],
        out_shape=[sds((t, d), BF16), sds((t, d), BF16), sds((t, d), BF16), sds((t, d), F32),
                   sds((t, IDX_HEADS * LANES), BF16), sds((t, LANES), BF16), sds((t, LANES), F32)],
        compiler_params=pltpu.CompilerParams(dimension_semantics=("arbitrary",),
                                             vmem_limit_bytes=VMEM_LIMIT),
        name="proj_b",
    )(x2, g, w_b, cosd, sind, cosi, sini, gq, gk)


def _sortable_key(x):
    bits = pltpu.bitcast(x, jnp.int32)
    return bits ^ ((bits >> 31) & jnp.int32(0x7FFFFFFF))


def _dsa_kernel(q_ref, qi_ref, wi_ref, zb_ref, ki_ref, k_hbm, v_hbm, o_ref,
                kbuf, vbuf, key_ref, wb_ref, sem, *, top_k):
    QB, KT = Q_TILE, K_TILE
    halves = KT // LANES
    b = pl.program_id(0)
    j = pl.program_id(1)

    @pl.when(j == 0)
    def _():
        ck = pltpu.make_async_copy(k_hbm.at[b], kbuf, sem.at[0])
        cv = pltpu.make_async_copy(v_hbm.at[b], vbuf, sem.at[1])
        ck.start()
        cv.start()
        ck.wait()
        cv.wait()

    nt = (j * QB + QB + KT - 1) // KT
    t_row = j * QB + lax.broadcasted_iota(jnp.int32, (QB, 1), 0)
    vis_end = (t_row // VIS_CHUNK + 1) * VIS_CHUNK
    lane_pos = lax.broadcasted_iota(jnp.int32, (QB, KT), 1)

    for h in range(IDX_HEADS):
        wb_ref[h] = jnp.broadcast_to(wi_ref[:, h:h + 1], (QB, LANES))

    def score_body(t, carry):
        kt = ki_ref[t]
        sc = [jnp.zeros((QB, LANES), F32) for _ in range(halves)]
        for h in range(IDX_HEADS):
            lg = _dot_nt(qi_ref[:, h * LANES:(h + 1) * LANES], kt)
            wbh = wb_ref[h]
            for c in range(halves):
                sc[c] = sc[c] + jnp.maximum(lg[:, c * LANES:(c + 1) * LANES], 0.0) * wbh
        key = _sortable_key(jnp.concatenate(sc, axis=1))
        key_ref[t] = jnp.where(t * KT + lane_pos < vis_end, key, INT_MIN)
        return carry

    lax.fori_loop(0, nt, score_body, 0)

    def count_ge(thr_b):
        def body(t, acc):
            kk = key_ref[t]
            for c in range(halves):
                acc = acc + jnp.where(kk[:, c * LANES:(c + 1) * LANES] >= thr_b, 1.0, 0.0)
            return acc
        acc = lax.fori_loop(0, nt, body, jnp.zeros((QB, LANES), F32))
        return jnp.sum(acc, axis=-1, keepdims=True)

    def bit_body(i, thr_u):
        cand_u = thr_u | lax.shift_left(jnp.int32(1), 31 - i)
        cnt = count_ge(jnp.broadcast_to(cand_u ^ INT_MIN, (QB, LANES)))
        return jnp.where(cnt >= float(top_k), cand_u, thr_u)

    thr_u = lax.fori_loop(0, 32, bit_body, jnp.zeros((QB, 1), jnp.int32))
    thr = thr_u ^ INT_MIN
    thr_b = jnp.broadcast_to(thr, (QB, LANES))
    thr_t = jnp.broadcast_to(thr, (QB, KT))

    n_ge = count_ge(thr_b)
    n_gt = count_ge(thr_b + 1) * jnp.where(thr == jnp.int32(2 ** 31 - 1), 0.0, 1.0)
    need = float(top_k) - n_gt
    surplus = jnp.max(n_ge - n_gt - need)

    @pl.when(surplus > 0.0)
    def _():
        r_i = lax.broadcasted_iota(jnp.int32, (KT, KT), 0)
        c_i = lax.broadcasted_iota(jnp.int32, (KT, KT), 1)
        prefix = jnp.where(r_i <= c_i, 1.0, 0.0).astype(BF16)
        demoted = jnp.where(thr_t == INT_MIN, INT_MIN, thr_t - 1)

        def tie_body(t, seen):
            kk = key_ref[t]
            eq = kk == thr_t
            eqf = jnp.where(eq, 1.0, 0.0)
            incl = _dot(eqf.astype(BF16), prefix)
            rank = seen + incl - eqf
            key_ref[t] = jnp.where(eq & (rank >= need), demoted, kk)
            return seen + incl[:, KT - 1:KT]

        lax.fori_loop(0, nt, tie_body, jnp.zeros((QB, 1), F32))

    def bias_body(t, carry):
        kk = key_ref[t]
        keep = jnp.where(t * KT + lane_pos < vis_end, jnp.where(kk >= thr_t, 0.0, NEG_BIG), NEG_BIG)
        key_ref[t] = pltpu.bitcast(keep.astype(F32), jnp.int32)
        return carry

    lax.fori_loop(0, nt, bias_body, 0)

    for h in range(DSA_HEADS):
        sl = slice(h * DSA_HD, (h + 1) * DSA_HD)
        qh = q_ref[:, sl]

        def attn_body(t, carry):
            m, l, acc = carry
            s = _dot_nt(qh, kbuf[t, :, sl]) + pltpu.bitcast(key_ref[t], F32)
            m_new = jnp.maximum(m, jnp.max(s, axis=-1, keepdims=True))
            alpha = jnp.exp(m - m_new)
            p = jnp.exp(s - m_new)
            l = alpha * l + jnp.sum(p, axis=-1, keepdims=True)
            acc = alpha * acc + _dot(p.astype(BF16), vbuf[t, :, sl])
            return m_new, l, acc

        init = (jnp.full((QB, 1), NEG_BIG, F32), jnp.zeros((QB, 1), F32), jnp.zeros((QB, DSA_HD), F32))
        _, l, acc = lax.fori_loop(0, nt, attn_body, init)
        o_ref[:, sl] = (acc / l * zb_ref[:, sl]).astype(BF16)


def _dsa(q, k, v, zb, qi, ki, wi, batch):
    t = q.shape[0]
    s = t // batch
    d = DSA_HEADS * DSA_HD
    QB, KT = Q_TILE, K_TILE
    nb = s // QB
    nkt = s // KT
    top_k = min(DSA_TOPK, s // 4)
    k4 = k.reshape(batch, nkt, KT, d)
    v4 = v.reshape(batch, nkt, KT, d)
    ki4 = ki.reshape(batch, nkt, KT, LANES)
    row = lambda n: pl.BlockSpec((QB, n), lambda b, j: (b * nb + j, 0))
    return pl.pallas_call(
        functools.partial(_dsa_kernel, top_k=top_k),
        grid=(batch, nb),
        in_specs=[row(d), row(IDX_HEADS * LANES), row(LANES), row(d),
                  pl.BlockSpec((None, nkt, KT, LANES), lambda b, j: (b, 0, 0, 0)),
                  pl.BlockSpec(memory_space=pl.ANY), pl.BlockSpec(memory_space=pl.ANY)],
        out_specs=row(d),
        out_shape=jax.ShapeDtypeStruct((t, d), BF16),
        scratch_shapes=[pltpu.VMEM((nkt, KT, d), BF16), pltpu.VMEM((nkt, KT, d), BF16),
                        pltpu.VMEM((nkt, QB, KT), jnp.int32),
                        pltpu.VMEM((IDX_HEADS, QB, LANES), F32),
                        pltpu.SemaphoreType.DMA((2,))],
        compiler_params=pltpu.CompilerParams(dimension_semantics=("arbitrary", "arbitrary"),
                                             vmem_limit_bytes=VMEM_LIMIT),
        name="dsa",
    )(q, qi, wi, zb, ki4, k4, v4)


def _mem_kv_kernel(mem_ref, g_ref, w_ref, gk_ref, km_ref, vm_ref):
    d = MEM_HEADS * MEM_HD
    mh = _rms_rows(mem_ref[...], g_ref[...]).astype(BF16)
    kv = _dot(mh, w_ref[...])
    for h in range(MEM_HEADS):
        sl = slice(h * MEM_HD, (h + 1) * MEM_HD)
        km_ref[:, sl] = _rms_rows(kv[:, sl], gk_ref[...]).astype(BF16)
    vm_ref[...] = kv[:, d:2 * d].astype(BF16)


def _mem_kv(mem, g, w, gk):
    batch, nm, _ = mem.shape
    d = MEM_HEADS * MEM_HD
    full = lambda a: pl.BlockSpec(a.shape, lambda b: (0, 0))
    blk = lambda n: pl.BlockSpec((None, nm, n), lambda b: (b, 0, 0))
    return pl.pallas_call(
        _mem_kv_kernel,
        grid=(batch,),
        in_specs=[blk(D_MODEL), full(g), full(w), full(gk)],
        out_specs=[blk(d), blk(d)],
        out_shape=[jax.ShapeDtypeStruct((batch, nm, d), BF16)] * 2,
        compiler_params=pltpu.CompilerParams(dimension_semantics=("arbitrary",),
                                             vmem_limit_bytes=VMEM_LIMIT),
        name="mem_kv",
    )(mem, g, w, gk)


def _mem_attn_kernel(x_ref, g_ref, w_ref, gq_ref, km_ref, vm_ref, y_ref):
    d = MEM_HEADS * MEM_HD
    h = _normed_input(x_ref, g_ref)
    qm = _dot(h, w_ref[:, 0:d])
    zm = _dot(h, w_ref[:, d:2 * d])
    for hd in range(MEM_HEADS):
        sl = slice(hd * MEM_HD, (hd + 1) * MEM_HD)
        qs = (_rms_rows(qm[:, sl], gq_ref[...]) * (MEM_HD ** -0.5)).astype(BF16)
        s = _dot_nt(qs, km_ref[:, sl])
        p = jnp.exp(s - jnp.max(s, axis=-1, keepdims=True))
        o = _dot(p.astype(BF16), vm_ref[:, sl]) / jnp.sum(p, axis=-1, keepdims=True)
        y_ref[:, sl] = (o * _silu(zm[:, sl])).astype(BF16)


def _mem_attn(x2, g, w_m, gq, km, vm):
    t = x2.shape[0]
    batch, nm, d = km.shape
    tm = ROW_TILE
    per_b = t // batch // tm
    row = lambda n: pl.BlockSpec((tm, n), lambda i: (i, 0))
    full = lambda a: pl.BlockSpec(a.shape, lambda i: (0, 0))
    mem = pl.BlockSpec((None, nm, d), lambda i: (i // per_b, 0, 0))
    return pl.pallas_call(
        _mem_attn_kernel,
        grid=(t // tm,),
        in_specs=[row(D_MODEL), full(g), full(w_m), full(gq), mem, mem],
        out_specs=row(d),
        out_shape=jax.ShapeDtypeStruct((t, d), BF16),
        compiler_params=pltpu.CompilerParams(dimension_semantics=("arbitrary",),
                                             vmem_limit_bytes=VMEM_LIMIT),
        name="mem_attn",
    )(x2, g, w_m, gq, km, vm)


def _merge_kernel(x_ref, g_ref, wg_ref, ya_ref, yb_ref, ym_ref, wb_ref, wo_ref, o_ref):
    x = x_ref[...]
    h = _rms_rows(x, g_ref[...]).astype(BF16)
    merged = None
    for i, y_ref in enumerate((ya_ref, yb_ref, ym_ref)):
        gate = _sigmoid(_dot(h, wg_ref[:, i * D_MODEL:(i + 1) * D_MODEL]))
        term = gate * _dot(y_ref[...], wb_ref[i])
        merged = term if merged is None else merged + term
    o_ref[...] = x + _dot(merged.astype(BF16), wo_ref[...])


def _merge(x2, g, w_gate, ya, yb, ym, w_branch, w_out):
    t = x2.shape[0]
    tm = ROW_TILE
    row = lambda n: pl.BlockSpec((tm, n), lambda i: (i, 0))
    full = lambda a: pl.BlockSpec(a.shape, lambda i: (0,) * a.ndim)
    return pl.pallas_call(
        _merge_kernel,
        grid=(t // tm,),
        in_specs=[row(D_MODEL), full(g), full(w_gate), row(D_MODEL), row(D_MODEL), row(D_MODEL),
                  full(w_branch), full(w_out)],
        out_specs=row(D_MODEL),
        out_shape=jax.ShapeDtypeStruct((t, D_MODEL), F32),
        compiler_params=pltpu.CompilerParams(dimension_semantics=("arbitrary",),
                                             vmem_limit_bytes=VMEM_LIMIT),
        name="merge",
    )(x2, g, w_gate, ya, yb, ym, w_branch, w_out)


def _rope_tables(positions, head_dim, pad_to):
    half = head_dim // 2
    inv_freq = jnp.power(ROPE_THETA, -jnp.arange(half, dtype=F32) / half)
    ang = positions.astype(F32)[..., None] * inv_freq
    cos, sin = jnp.cos(ang), jnp.sin(ang)
    pad = jnp.zeros(cos.shape[:-1] + (pad_to // 2 - half,), F32)
    cos_t = jnp.concatenate([cos, pad, cos, pad], axis=-1)
    sin_t = jnp.concatenate([-sin, pad, sin, pad], axis=-1)
    return cos_t.reshape(-1, pad_to), sin_t.reshape(-1, pad_to)


def _pad_idx_cols(w, heads):
    dm = w.shape[0]
    half = IDX_HD // 2
    w4 = w.reshape(dm, heads, 2, half)
    w4 = jnp.concatenate([w4, jnp.zeros_like(w4)], axis=-1)
    return w4.reshape(dm, heads * LANES)


def _pad_cols(w, n):
    return jnp.pad(w, ((0, 0), (0, n - w.shape[1])))


def _layer(x, mem, positions, norm_g, mem_norm_g, w_in, conv_w, conv_b, gate_b, mlstm_norm_g,
           dsa_q_norm_g, dsa_k_norm_g, mem_q_norm_g, mem_k_norm_g, w_mem_kv, w_branch, w_out):
    batch, seq, dm = x.shape
    x2 = x.reshape(batch * seq, dm)
    d = D_MODEL
    splits = (2 * d, d, 2 * MLSTM_HEADS, d, d, d, d, d, d, IDX_HEADS * IDX_HD, IDX_HD, IDX_HEADS,
              d, d, N_BRANCH * d)
    cuts = np.cumsum(np.array(splits))[:-1].tolist()
    (w_qk, w_va, w_if, w_oa, w_za, w_qb, w_kb, w_vb, w_zb, w_qi, w_ki, w_wi,
     w_qm, w_zm, w_gate) = jnp.split(w_in, cuts, axis=-1)
    g = norm_g.reshape(1, dm)

    w_a = jnp.concatenate([w_qk, w_va, w_oa, w_za], axis=1).astype(BF16)
    qk, va, og, ifc = _proj_a(x2, g, w_a, _pad_cols(w_if, LANES).astype(BF16))
    ifr = ifc[:, :8].reshape(batch, seq, 8).transpose(0, 2, 1)
    gb = gate_b.reshape(2 * MLSTM_HEADS)
    ya = _mlstm(qk, va, og, ifc, ifr, conv_w, conv_b.reshape(1, -1),
                _pad_cols(gb.reshape(1, -1), LANES), gb.reshape(-1, 1),
                mlstm_norm_g.reshape(MLSTM_HEADS, 1, MLSTM_HD), batch)

    w_b = jnp.concatenate([w_qb, w_kb, w_vb, w_zb, _pad_idx_cols(w_qi, IDX_HEADS),
                           _pad_idx_cols(w_ki, 1), _pad_cols(w_wi, LANES)], axis=1).astype(BF16)
    cosd, sind = _rope_tables(positions, DSA_HD, LANES)
    cosi, sini = _rope_tables(positions, IDX_HD, LANES)
    qb, kb, vb, zb, qi, ki, wi = _proj_b(x2, g, w_b, cosd, sind, cosi, sini,
                                         dsa_q_norm_g.reshape(1, -1), dsa_k_norm_g.reshape(1, -1))
    yb = _dsa(qb, kb, vb, zb, qi, ki, wi, batch)

    km, vm = _mem_kv(mem, mem_norm_g.reshape(1, dm), w_mem_kv.astype(BF16), mem_k_norm_g.reshape(1, -1))
    ym = _mem_attn(x2, g, jnp.concatenate([w_qm, w_zm], axis=1).astype(BF16),
                   mem_q_norm_g.reshape(1, -1), km, vm)

    out = _merge(x2, g, w_gate.astype(BF16), ya, yb, ym, w_branch.astype(BF16), w_out.astype(BF16))
    return out.reshape(batch, seq, dm)


def kernel(x, mem, positions, norm_g, mem_norm_g, w_in, conv_w, conv_b, mlstm_gate_b, mlstm_norm_g,
           dsa_q_norm_g, dsa_k_norm_g, mem_q_norm_g, mem_k_norm_g, w_mem_kv, w_branch, w_out):
    for l in range(norm_g.shape[0]):
        x = _layer(x, mem, positions, norm_g[l], mem_norm_g[l], w_in[l], conv_w[l], conv_b[l],
                   mlstm_gate_b[l], mlstm_norm_g[l], dsa_q_norm_g[l], dsa_k_norm_g[l],
                   mem_q_norm_g[l], mem_k_norm_g[l], w_mem_kv[l], w_branch[l], w_out[l])
    return x
```

```python
import functools
import math

import numpy as np
import jax
import jax.numpy as jnp
from jax import lax
from jax.experimental import pallas as pl
from jax.experimental.pallas import tpu as pltpu

F32 = jnp.float32
BF16 = jnp.bfloat16
I16 = jnp.int16
I32 = jnp.int32

EPS = 1e-6
ROPE_THETA = 10000.0
D_MODEL = 1024
MLSTM_HEADS = 4
MLSTM_HD = 256
CONV_W = 4
DSA_HEADS = 8
DSA_HD = 128
IDX_HEADS = 8
IDX_HD = 64
DSA_TOPK = 256
VIS_CHUNK = 64
MEM_HEADS = 4
MEM_HD = 256
N_BRANCH = 3

LANES = 128
SUBLANES = 8
PACKED_ROWS = 16
SCAN_CHUNK = 256
DSA_TILE = 256
ROW_TILE = 256
NEG_BIG = -1e30
INT_MIN = -2 ** 31
LOG2E = math.log2(math.e)
VMEM_LIMIT = 56 * 1024 * 1024


def _sigmoid(x):
    return 1.0 / (1.0 + jnp.exp(-x))


def _silu(x):
    return x * _sigmoid(x)


def _log_sigmoid(x):
    return jnp.minimum(x, 0.0) - jnp.log(1.0 + jnp.exp(-jnp.abs(x)))


def _dot(a, b):
    return jnp.dot(a, b, preferred_element_type=F32)


def _dot_nt(a, b):
    return lax.dot_general(a, b, (((1,), (1,)), ((), ())), preferred_element_type=F32)


def _dot_tn(a, b):
    return lax.dot_general(a, b, (((0,), (0,)), ((), ())), preferred_element_type=F32)


def _rms_rows(x, g):
    ms = jnp.mean(x * x, axis=-1, keepdims=True)
    return x * lax.rsqrt(ms + EPS) * g


def _normed_input(x_ref, g_ref):
    return _rms_rows(x_ref[...], g_ref[...]).astype(BF16)


def _rotate(x, cos, sin_signed):
    return x * cos + pltpu.roll(x, LANES // 2, 1) * sin_signed


def _proj_a_kernel(x_ref, g_ref, w_ref, wif_ref, qk_ref, v_ref, og_ref, if_ref):
    d = MLSTM_HEADS * MLSTM_HD
    h = _normed_input(x_ref, g_ref)
    qk_ref[...] = _dot(h, w_ref[:, 0:2 * d])
    v_ref[...] = _dot(h, w_ref[:, 2 * d:3 * d]).astype(BF16)
    o = _dot(h, w_ref[:, 3 * d:4 * d])
    z = _dot(h, w_ref[:, 4 * d:5 * d])
    og_ref[...] = _sigmoid(o) * _silu(z)
    if_ref[...] = _dot(h, wif_ref[...])


def _proj_a(x2, g, w_a, w_if):
    t = x2.shape[0]
    d = MLSTM_HEADS * MLSTM_HD
    tm = ROW_TILE
    row = lambda n: pl.BlockSpec((tm, n), lambda i: (i, 0))
    full = lambda a: pl.BlockSpec(a.shape, lambda i: (0, 0))
    return pl.pallas_call(
        _proj_a_kernel,
        grid=(t // tm,),
        in_specs=[row(D_MODEL), full(g), full(w_a), full(w_if)],
        out_specs=[row(2 * d), row(d), row(d), row(LANES)],
        out_shape=[jax.ShapeDtypeStruct((t, 2 * d), F32), jax.ShapeDtypeStruct((t, d), BF16),
                   jax.ShapeDtypeStruct((t, d), F32), jax.ShapeDtypeStruct((t, LANES), F32)],
        compiler_params=pltpu.CompilerParams(dimension_semantics=("arbitrary",),
                                             vmem_limit_bytes=VMEM_LIMIT),
        name="proj_a",
    )(x2, g, w_a, w_if)


def _mlstm_kernel(qk_ref, v_ref, og_ref, ifc_ref, ifr_ref, cw_ref, cb_ref, gbc_ref, gbr_ref, ng_ref,
                  y_ref, ext_ref, c_ref, n_ref, m_ref):
    L = SCAN_CHUNK
    dh = MLSTM_HD
    d = MLSTM_HEADS * dh
    halo = 8

    @pl.when(pl.program_id(1) == 0)
    def _():
        ext_ref[0:halo, :] = jnp.zeros((halo, 2 * d), F32)
        c_ref[...] = jnp.zeros_like(c_ref)
        n_ref[...] = jnp.zeros_like(n_ref)
        m_ref[...] = jnp.zeros_like(m_ref)

    ext_ref[halo:halo + L, :] = qk_ref[...]

    def conv_silu(c0):
        u = cb_ref[:, c0:c0 + dh]
        for j in range(CONV_W):
            r0 = halo - (CONV_W - 1) + j
            u = u + cw_ref[j:j + 1, c0:c0 + dh] * ext_ref[r0:r0 + L, c0:c0 + dh]
        return _silu(u)

    row_i = lax.broadcasted_iota(I32, (L, L), 0)
    col_i = lax.broadcasted_iota(I32, (L, L), 1)
    causal = row_i >= col_i
    tril = jnp.where(causal, 1.0, 0.0).astype(F32)
    triu = jnp.where(col_i >= row_i, 1.0, 0.0).astype(F32)

    ifc = ifc_ref[...] + gbc_ref[...]
    ifr = ifr_ref[...] + gbr_ref[...]
    bcol_all = jnp.dot(tril, _log_sigmoid(ifc), preferred_element_type=F32,
                       precision=lax.Precision.HIGHEST)
    brow_all = jnp.dot(_log_sigmoid(ifr), triu, preferred_element_type=F32,
                       precision=lax.Precision.HIGHEST)

    for h in range(MLSTM_HEADS):
        qf = conv_silu(h * dh)
        kf = conv_silu(d + h * dh) * (dh ** -0.5)
        q = qf.astype(BF16)
        v = v_ref[:, h * dh:(h + 1) * dh]
        icol = ifc[:, h:h + 1]
        irow = ifr[h:h + 1, :]
        bcol = bcol_all[:, MLSTM_HEADS + h:MLSTM_HEADS + h + 1]
        brow = brow_all[MLSTM_HEADS + h:MLSTM_HEADS + h + 1, :]
        m_prev = m_ref[h:h + 1, 0:1]

        d_log = jnp.where(causal, bcol - brow + irow, -jnp.inf)
        inter = bcol + m_prev
        m_t = jnp.maximum(inter, jnp.max(d_log, axis=-1, keepdims=True))
        w_intra = jnp.exp(d_log - m_t)
        w_inter = jnp.exp(inter - m_t)
        a = _dot_nt(q, kf.astype(BF16)) * w_intra
        nrow = n_ref[h]
        num = w_inter * _dot(q, c_ref[h].astype(BF16)) + _dot(a.astype(BF16), v)
        den = w_inter * jnp.sum(qf * nrow, axis=-1, keepdims=True) + jnp.sum(a, axis=-1, keepdims=True)
        hh = num / jnp.maximum(jnp.abs(den), jnp.exp(-m_t))
        hn = _rms_rows(hh, ng_ref[h])
        y_ref[:, h * dh:(h + 1) * dh] = (hn * og_ref[:, h * dh:(h + 1) * dh]).astype(BF16)

        b_last = bcol[L - 1:L, :]
        g_col = b_last - bcol + icol
        m_new = jnp.maximum(b_last + m_prev, jnp.max(g_col, axis=0, keepdims=True))
        decay = jnp.exp(b_last + m_prev - m_new)
        kw = kf * jnp.exp(g_col - m_new)
        c_ref[h] = decay * c_ref[h] + _dot_tn(kw.astype(BF16), v)
        n_ref[h] = decay * nrow + jnp.sum(kw, axis=0, keepdims=True)
        m_ref[h:h + 1, :] = jnp.broadcast_to(m_new, (1, LANES))

    ext_ref[0:halo, :] = ext_ref[L:L + halo, :]


def _mlstm(qk, v, og, ifc, ifr, conv_w, conv_b, gate_col, gate_row, norm_g, batch):
    t = qk.shape[0]
    L = SCAN_CHUNK
    d = MLSTM_HEADS * MLSTM_HD
    nc = t // batch // L
    row = lambda n: pl.BlockSpec((L, n), lambda b, c: (b * nc + c, 0))
    full = lambda a: pl.BlockSpec(a.shape, lambda b, c: (0,) * a.ndim)
    return pl.pallas_call(
        _mlstm_kernel,
        grid=(batch, nc),
        in_specs=[row(2 * d), row(d), row(d), row(LANES),
                  pl.BlockSpec((None, 8, L), lambda b, c: (b, 0, c)),
                  full(conv_w), full(conv_b), full(gate_col), full(gate_row), full(norm_g)],
        out_specs=row(d),
        out_shape=jax.ShapeDtypeStruct((t, d), BF16),
        scratch_shapes=[pltpu.VMEM((L + 8, 2 * d), F32),
                        pltpu.VMEM((MLSTM_HEADS, MLSTM_HD, MLSTM_HD), F32),
                        pltpu.VMEM((MLSTM_HEADS, 1, MLSTM_HD), F32),
                        pltpu.VMEM((8, LANES), F32)],
        compiler_params=pltpu.CompilerParams(dimension_semantics=("arbitrary", "arbitrary"),
                                             vmem_limit_bytes=VMEM_LIMIT),
        name="mlstm",
    )(qk, v, og, ifc, ifr, conv_w, conv_b, gate_col, gate_row, norm_g)


def _proj_b_kernel(x_ref, g_ref, w_ref, cosd_ref, sind_ref, cosi_ref, sini_ref, gq_ref, gk_ref,
                   qt_ref, k_ref, vt_ref, z_ref, qit_ref, ki_ref, wit_ref):
    d = DSA_HEADS * DSA_HD
    h = _normed_input(x_ref, g_ref)
    cosd = cosd_ref[...]
    sind = sind_ref[...]
    cosi = cosi_ref[...]
    sini = sini_ref[...]

    acc = _dot(h, w_ref[:, 0:d])
    for hd in range(DSA_HEADS):
        sl = slice(hd * DSA_HD, (hd + 1) * DSA_HD)
        xs = _rotate(_rms_rows(acc[:, sl], gq_ref[...]), cosd, sind) * (DSA_HD ** -0.5 * LOG2E)
        qt_ref[sl, :] = xs.T.astype(BF16)
    acc = _dot(h, w_ref[:, d:2 * d])
    for hd in range(DSA_HEADS):
        sl = slice(hd * DSA_HD, (hd + 1) * DSA_HD)
        k_ref[:, sl] = _rotate(_rms_rows(acc[:, sl], gk_ref[...]), cosd, sind).astype(BF16)
    acc = _dot(h, w_ref[:, 2 * d:3 * d])
    for hd in range(DSA_HEADS):
        sl = slice(hd * DSA_HD, (hd + 1) * DSA_HD)
        vt_ref[sl, :] = acc[:, sl].T.astype(BF16)
    z_ref[...] = _silu(_dot(h, w_ref[:, 3 * d:4 * d]))
    acc = _dot(h, w_ref[:, 4 * d:4 * d + IDX_HEADS * LANES])
    for hd in range(IDX_HEADS):
        sl = slice(hd * LANES, (hd + 1) * LANES)
        qit_ref[sl, :] = (_rotate(acc[:, sl], cosi, sini) * (IDX_HD ** -0.5)).T.astype(BF16)
    c0 = 4 * d + IDX_HEADS * LANES
    ki_ref[...] = _rotate(_dot(h, w_ref[:, c0:c0 + LANES]), cosi, sini).astype(BF16)
    wi = _dot(h, w_ref[:, c0 + LANES:c0 + 2 * LANES]) * (IDX_HEADS ** -0.5)
    wit_ref[...] = wi.T[0:IDX_HEADS, :]


def _proj_b(x2, g, w_b, cosd, sind, cosi, sini, gq, gk):
    t = x2.shape[0]
    d = DSA_HEADS * DSA_HD
    tm = DSA_TILE
    nt = t // tm
    row = lambda n: pl.BlockSpec((tm, n), lambda i: (i, 0))
    full = lambda a: pl.BlockSpec(a.shape, lambda i: (0, 0))
    fmaj = lambda n: pl.BlockSpec((None, n, tm), lambda i: (i, 0, 0))
    sds = jax.ShapeDtypeStruct
    return pl.pallas_call(
        _proj_b_kernel,
        grid=(nt,),
        in_specs=[row(D_MODEL), full(g), full(w_b), row(LANES), row(LANES), row(LANES), row(LANES),
                  full(gq), full(gk)],
        out_specs=[fmaj(d), row(d), fmaj(d), row(d), fmaj(IDX_HEADS * LANES), row(LANES), fmaj(IDX_HEADS)],
        out_shape=[sds((nt, d, tm), BF16), sds((t, d), BF16), sds((nt, d, tm), BF16), sds((t, d), F32),
                   sds((nt, IDX_HEADS * LANES, tm), BF16), sds((t, LANES), BF16),
                   sds((nt, IDX_HEADS, tm), F32)],
        compiler_params=pltpu.CompilerParams(dimension_semantics=("arbitrary",),
                                             vmem_limit_bytes=VMEM_LIMIT),
        name="proj_b",
    )(x2, g, w_b, cosd, sind, cosi, sini, gq, gk)


def _sortable_key(x):
    bits = pltpu.bitcast(x, I32)
    return bits ^ ((bits >> 31) & jnp.int32(0x7FFFFFFF))


def _fold_rows(x, rows, op=jnp.add):
    acc = x[0:rows, :]
    for r in range(1, x.shape[0] // rows):
        acc = op(acc, x[r * rows:(r + 1) * rows, :])
    return acc


def _dsa_kernel(qt_ref, qit_ref, wit_ref, zb_ref, ki_ref, k_hbm, vt_hbm, o_ref,
                kbuf, vbuf, key_ref, k16_ref, acc_ref, m_ref, l_ref, s_ref, mt_ref, sem, *, top_k):
    T = DSA_TILE
    b = pl.program_id(0)
    j = pl.program_id(1)

    @pl.when(j == 0)
    def _():
        ck = pltpu.make_async_copy(k_hbm.at[b], kbuf, sem.at[0])
        cv = pltpu.make_async_copy(vt_hbm.at[b], vbuf, sem.at[1])
        ck.start()
        cv.start()
        ck.wait()
        cv.wait()

    nt = j + 1
    t_q = j * T + lax.broadcasted_iota(I32, (1, T), 1)
    vis_end = (t_q // VIS_CHUNK + 1) * VIS_CHUNK
    row_pos = lax.broadcasted_iota(I32, (T, T), 0)

    def score_body(t, carry):
        kt = ki_ref[t]
        sc = jnp.zeros((T, T), F32)
        for h in range(IDX_HEADS):
            lg = _dot(kt, qit_ref[h * LANES:(h + 1) * LANES, :])
            sc = sc + jnp.maximum(lg, 0.0) * wit_ref[h:h + 1, :]
        key = jnp.where(t * T + row_pos < vis_end, _sortable_key(sc), INT_MIN)
        key_ref[t] = key
        k16_ref[t] = (key >> 16).astype(I16)
        return carry

    lax.fori_loop(0, nt, score_body, 0)

    def count16_ge(cand_row):
        cand = jnp.broadcast_to(cand_row.astype(I16), (PACKED_ROWS, T))

        def body(t, acc):
            for r in range(T // PACKED_ROWS):
                blk = k16_ref[t, r * PACKED_ROWS:(r + 1) * PACKED_ROWS, :]
                acc = acc + jnp.where(blk >= cand, jnp.int16(1), jnp.int16(0))
            return acc

        acc = lax.fori_loop(0, nt, body, jnp.zeros((PACKED_ROWS, T), I16))
        return jnp.sum(acc.astype(I32), axis=0, keepdims=True)

    def bisect16(need_row):
        def bit_body(i, thr_u):
            cand_u = thr_u | lax.shift_left(jnp.int32(1), 15 - i)
            cnt = count16_ge(cand_u - 32768)
            return jnp.where(cnt >= need_row, cand_u, thr_u)
        return lax.fori_loop(0, 16, bit_body, jnp.zeros((1, T), I32)) - 32768

    thr_hi = bisect16(jnp.full((1, T), top_k, I32))
    n_above = count16_ge(thr_hi + 1) * jnp.where(thr_hi == 32767, 0, 1)

    def low_body(t, carry):
        key = key_ref[t]
        low = (key & 0xFFFF) - 32768
        k16_ref[t] = jnp.where((key >> 16) == thr_hi, low, -32768).astype(I16)
        return carry

    lax.fori_loop(0, nt, low_body, 0)
    thr_lo = bisect16(top_k - n_above)
    thr = thr_hi * 65536 + (thr_lo + 32768)

    def count32_body(t, carry):
        n_ge, n_gt = carry
        key = key_ref[t]
        n_ge = n_ge + _fold_rows(jnp.where(key >= thr, 1.0, 0.0), SUBLANES)
        n_gt = n_gt + _fold_rows(jnp.where(key > thr, 1.0, 0.0), SUBLANES)
        return n_ge, n_gt

    zero8 = jnp.zeros((SUBLANES, T), F32)
    n_ge, n_gt = lax.fori_loop(0, nt, count32_body, (zero8, zero8))
    n_ge = jnp.sum(n_ge, axis=0, keepdims=True)
    n_gt = jnp.sum(n_gt, axis=0, keepdims=True)
    need = float(top_k) - n_gt
    surplus = jnp.max(n_ge - n_gt - need)

    @pl.when(surplus > 0.0)
    def _():
        c_i = lax.broadcasted_iota(I32, (T, T), 1)
        prefix = jnp.where(c_i <= row_pos, 1.0, 0.0).astype(BF16)
        demoted = jnp.where(thr == INT_MIN, INT_MIN, thr - 1)

        def tie_body(t, seen):
            key = key_ref[t]
            eq = key == thr
            eqf = jnp.where(eq, 1.0, 0.0)
            incl = _dot(prefix, eqf.astype(BF16))
            rank = seen + incl - eqf
            key_ref[t] = jnp.where(eq & (rank >= need), demoted, key)
            return seen + incl[T - 1:T, :]

        lax.fori_loop(0, nt, tie_body, jnp.zeros((1, T), F32))

    def bias_body(t, carry):
        key = key_ref[t]
        keep = jnp.where(t * T + row_pos < vis_end, jnp.where(key >= thr, 0.0, NEG_BIG), NEG_BIG)
        key_ref[t] = pltpu.bitcast(keep.astype(F32), I32)
        return carry

    lax.fori_loop(0, nt, bias_body, 0)

    m_ref[...] = jnp.full(m_ref.shape, NEG_BIG, F32)
    l_ref[...] = jnp.zeros_like(l_ref)
    acc_ref[...] = jnp.zeros_like(acc_ref)

    def attn_body(t, carry):
        bias = pltpu.bitcast(key_ref[t], F32)
        for h in range(DSA_HEADS):
            sl = slice(h * DSA_HD, (h + 1) * DSA_HD)
            s = _dot(kbuf[t, :, sl], qt_ref[sl, :]) + bias
            s_ref[h] = s
            mt_ref[h] = _fold_rows(s, SUBLANES, jnp.maximum)
        for h in range(DSA_HEADS):
            sl = slice(h * DSA_HD, (h + 1) * DSA_HD)
            m_old = m_ref[h]
            m_new = jnp.maximum(m_old, jnp.max(mt_ref[h], axis=0, keepdims=True))
            alpha = jnp.exp2(m_old - m_new)
            p = jnp.exp2(s_ref[h] - m_new)
            l_ref[h] = alpha * l_ref[h] + _fold_rows(p, SUBLANES)
            acc_ref[sl, :] = alpha * acc_ref[sl, :] + _dot(vbuf[t, sl, :], p.astype(BF16))
            m_ref[h] = m_new
        return carry

    lax.fori_loop(0, nt, attn_body, 0)

    for h in range(DSA_HEADS):
        sl = slice(h * DSA_HD, (h + 1) * DSA_HD)
        out_t = acc_ref[sl, :] / jnp.sum(l_ref[h], axis=0, keepdims=True)
        o_ref[:, sl] = (out_t.T * zb_ref[:, sl]).astype(BF16)


def _dsa(qt, k, vt, zb, qit, ki, wit, batch):
    t, d = k.shape
    s = t // batch
    T = DSA_TILE
    nb = s // T
    top_k = min(DSA_TOPK, s // 4)
    k4 = k.reshape(batch, nb, T, d)
    vt4 = vt.reshape(batch, nb, d, T)
    ki4 = ki.reshape(batch, nb, T, LANES)
    fmaj = lambda n: pl.BlockSpec((None, n, T), lambda b, j: (b * nb + j, 0, 0))
    row = lambda n: pl.BlockSpec((T, n), lambda b, j: (b * nb + j, 0))
    return pl.pallas_call(
        functools.partial(_dsa_kernel, top_k=top_k),
        grid=(batch, nb),
        in_specs=[fmaj(d), fmaj(IDX_HEADS * LANES), fmaj(IDX_HEADS), row(d),
                  pl.BlockSpec((None, nb, T, LANES), lambda b, j: (b, 0, 0, 0)),
                  pl.BlockSpec(memory_space=pl.ANY), pl.BlockSpec(memory_space=pl.ANY)],
        out_specs=row(d),
        out_shape=jax.ShapeDtypeStruct((t, d), BF16),
        scratch_shapes=[pltpu.VMEM((nb, T, d), BF16), pltpu.VMEM((nb, d, T), BF16),
                        pltpu.VMEM((nb, T, T), I32), pltpu.VMEM((nb, T, T), I16),
                        pltpu.VMEM((d, T), F32), pltpu.VMEM((DSA_HEADS, 1, T), F32),
                        pltpu.VMEM((DSA_HEADS, SUBLANES, T), F32),
                        pltpu.VMEM((DSA_HEADS, T, T), F32),
                        pltpu.VMEM((DSA_HEADS, SUBLANES, T), F32),
                        pltpu.SemaphoreType.DMA((2,))],
        compiler_params=pltpu.CompilerParams(dimension_semantics=("arbitrary", "arbitrary"),
                                             vmem_limit_bytes=VMEM_LIMIT),
        name="dsa",
    )(qt, qit, wit, zb, ki4, k4, vt4)


def _mem_kv_kernel(mem_ref, g_ref, w_ref, gk_ref, km_ref, vm_ref):
    d = MEM_HEADS * MEM_HD
    mh = _rms_rows(mem_ref[...], g_ref[...]).astype(BF16)
    kv = _dot(mh, w_ref[...])
    for h in range(MEM_HEADS):
        sl = slice(h * MEM_HD, (h + 1) * MEM_HD)
        km_ref[:, sl] = _rms_rows(kv[:, sl], gk_ref[...]).astype(BF16)
    vm_ref[...] = kv[:, d:2 * d].astype(BF16)


def _mem_kv(mem, g, w, gk):
    batch, nm, _ = mem.shape
    d = MEM_HEADS * MEM_HD
    full = lambda a: pl.BlockSpec(a.shape, lambda b: (0, 0))
    blk = lambda n: pl.BlockSpec((None, nm, n), lambda b: (b, 0, 0))
    return pl.pallas_call(
        _mem_kv_kernel,
        grid=(batch,),
        in_specs=[blk(D_MODEL), full(g), full(w), full(gk)],
        out_specs=[blk(d), blk(d)],
        out_shape=[jax.ShapeDtypeStruct((batch, nm, d), BF16)] * 2,
        compiler_params=pltpu.CompilerParams(dimension_semantics=("arbitrary",),
                                             vmem_limit_bytes=VMEM_LIMIT),
        name="mem_kv",
    )(mem, g, w, gk)


def _mem_attn_kernel(x_ref, g_ref, w_ref, gq_ref, km_ref, vm_ref, y_ref):
    d = MEM_HEADS * MEM_HD
    h = _normed_input(x_ref, g_ref)
    qm = _dot(h, w_ref[:, 0:d])
    zm = _dot(h, w_ref[:, d:2 * d])
    for hd in range(MEM_HEADS):
        sl = slice(hd * MEM_HD, (hd + 1) * MEM_HD)
        qs = (_rms_rows(qm[:, sl], gq_ref[...]) * (MEM_HD ** -0.5)).astype(BF16)
        s = _dot_nt(qs, km_ref[:, sl])
        p = jnp.exp(s - jnp.max(s, axis=-1, keepdims=True))
        o = _dot(p.astype(BF16), vm_ref[:, sl]) / jnp.sum(p, axis=-1, keepdims=True)
        y_ref[:, sl] = (o * _silu(zm[:, sl])).astype(BF16)


def _mem_attn(x2, g, w_m, gq, km, vm):
    t = x2.shape[0]
    batch, nm, d = km.shape
    tm = ROW_TILE
    per_b = t // batch // tm
    row = lambda n: pl.BlockSpec((tm, n), lambda i: (i, 0))
    full = lambda a: pl.BlockSpec(a.shape, lambda i: (0, 0))
    mem = pl.BlockSpec((None, nm, d), lambda i: (i // per_b, 0, 0))
    return pl.pallas_call(
        _mem_attn_kernel,
        grid=(t // tm,),
        in_specs=[row(D_MODEL), full(g), full(w_m), full(gq), mem, mem],
        out_specs=row(d),
        out_shape=jax.ShapeDtypeStruct((t, d), BF16),
        compiler_params=pltpu.CompilerParams(dimension_semantics=("arbitrary",),
                                             vmem_limit_bytes=VMEM_LIMIT),
        name="mem_attn",
    )(x2, g, w_m, gq, km, vm)


def _merge_kernel(x_ref, g_ref, wg_ref, ya_ref, yb_ref, ym_ref, wb_ref, wo_ref, o_ref):
    x = x_ref[...]
    h = _rms_rows(x, g_ref[...]).astype(BF16)
    merged = None
    for i, y_ref in enumerate((ya_ref, yb_ref, ym_ref)):
        gate = _sigmoid(_dot(h, wg_ref[:, i * D_MODEL:(i + 1) * D_MODEL]))
        term = gate * _dot(y_ref[...], wb_ref[i])
        merged = term if merged is None else merged + term
    o_ref[...] = x + _dot(merged.astype(BF16), wo_ref[...])


def _merge(x2, g, w_gate, ya, yb, ym, w_branch, w_out):
    t = x2.shape[0]
    tm = ROW_TILE
    row = lambda n: pl.BlockSpec((tm, n), lambda i: (i, 0))
    full = lambda a: pl.BlockSpec(a.shape, lambda i: (0,) * a.ndim)
    return pl.pallas_call(
        _merge_kernel,
        grid=(t // tm,),
        in_specs=[row(D_MODEL), full(g), full(w_gate), row(D_MODEL), row(D_MODEL), row(D_MODEL),
                  full(w_branch), full(w_out)],
        out_specs=row(D_MODEL),
        out_shape=jax.ShapeDtypeStruct((t, D_MODEL), F32),
        compiler_params=pltpu.CompilerParams(dimension_semantics=("arbitrary",),
                                             vmem_limit_bytes=VMEM_LIMIT),
        name="merge",
    )(x2, g, w_gate, ya, yb, ym, w_branch, w_out)


def _rope_tables(positions, head_dim, pad_to):
    half = head_dim // 2
    inv_freq = jnp.power(ROPE_THETA, -jnp.arange(half, dtype=F32) / half)
    ang = positions.astype(F32)[..., None] * inv_freq
    cos, sin = jnp.cos(ang), jnp.sin(ang)
    pad = jnp.zeros(cos.shape[:-1] + (pad_to // 2 - half,), F32)
    cos_t = jnp.concatenate([cos, pad, cos, pad], axis=-1)
    sin_t = jnp.concatenate([-sin, pad, sin, pad], axis=-1)
    return cos_t.reshape(-1, pad_to), sin_t.reshape(-1, pad_to)


def _pad_idx_cols(w, heads):
    dm = w.shape[0]
    half = IDX_HD // 2
    w4 = w.reshape(dm, heads, 2, half)
    w4 = jnp.concatenate([w4, jnp.zeros_like(w4)], axis=-1)
    return w4.reshape(dm, heads * LANES)


def _pad_cols(w, n):
    return jnp.pad(w, ((0, 0), (0, n - w.shape[1])))


def _layer(x, mem, positions, norm_g, mem_norm_g, w_in, conv_w, conv_b, gate_b, mlstm_norm_g,
           dsa_q_norm_g, dsa_k_norm_g, mem_q_norm_g, mem_k_norm_g, w_mem_kv, w_branch, w_out):
    batch, seq, dm = x.shape
    x2 = x.reshape(batch * seq, dm)
    d = D_MODEL
    splits = (2 * d, d, 2 * MLSTM_HEADS, d, d, d, d, d, d, IDX_HEADS * IDX_HD, IDX_HD, IDX_HEADS,
              d, d, N_BRANCH * d)
    cuts = np.cumsum(np.array(splits))[:-1].tolist()
    (w_qk, w_va, w_if, w_oa, w_za, w_qb, w_kb, w_vb, w_zb, w_qi, w_ki, w_wi,
     w_qm, w_zm, w_gate) = jnp.split(w_in, cuts, axis=-1)
    g = norm_g.reshape(1, dm)

    w_a = jnp.concatenate([w_qk, w_va, w_oa, w_za], axis=1).astype(BF16)
    qk, va, og, ifc = _proj_a(x2, g, w_a, _pad_cols(w_if, LANES).astype(BF16))
    ifr = ifc[:, :8].reshape(batch, seq, 8).transpose(0, 2, 1)
    gb = gate_b.reshape(2 * MLSTM_HEADS)
    ya = _mlstm(qk, va, og, ifc, ifr, conv_w, conv_b.reshape(1, -1),
                _pad_cols(gb.reshape(1, -1), LANES), gb.reshape(-1, 1),
                mlstm_norm_g.reshape(MLSTM_HEADS, 1, MLSTM_HD), batch)

    w_b = jnp.concatenate([w_qb, w_kb, w_vb, w_zb, _pad_idx_cols(w_qi, IDX_HEADS),
                           _pad_idx_cols(w_ki, 1), _pad_cols(w_wi, LANES)], axis=1).astype(BF16)
    cosd, sind = _rope_tables(positions, DSA_HD, LANES)
    cosi, sini = _rope_tables(positions, IDX_HD, LANES)
    qt, kb, vt, zb, qit, ki, wit = _proj_b(x2, g, w_b, cosd, sind, cosi, sini,
                                           dsa_q_norm_g.reshape(1, -1), dsa_k_norm_g.reshape(1, -1))
    yb = _dsa(qt, kb, vt, zb, qit, ki, wit, batch)

    km, vm = _mem_kv(mem, mem_norm_g.reshape(1, dm), w_mem_kv.astype(BF16), mem_k_norm_g.reshape(1, -1))
    ym = _mem_attn(x2, g, jnp.concatenate([w_qm, w_zm], axis=1).astype(BF16),
                   mem_q_norm_g.reshape(1, -1), km, vm)

    out = _merge(x2, g, w_gate.astype(BF16), ya, yb, ym, w_branch.astype(BF16), w_out.astype(BF16))
    return out.reshape(batch, seq, dm)


def kernel(x, mem, positions, norm_g, mem_norm_g, w_in, conv_w, conv_b, mlstm_gate_b, mlstm_norm_g,
           dsa_q_norm_g, dsa_k_norm_g, mem_q_norm_g, mem_k_norm_g, w_mem_kv, w_branch, w_out):
    for l in range(norm_g.shape[0]):
        x = _layer(x, mem, positions, norm_g[l], mem_norm_g[l], w_in[l], conv_w[l], conv_b[l],
                   mlstm_gate_b[l], mlstm_norm_g[l], dsa_q_norm_g[l], dsa_k_norm_g[l],
                   mem_q_norm_g[l], mem_k_norm_g[l], w_mem_kv[l], w_branch[l], w_out[l])
    return x
```

```python
import functools
import math

import numpy as np
import jax
import jax.numpy as jnp
from jax import lax
from jax.experimental import pallas as pl
from jax.experimental.pallas import tpu as pltpu

F32 = jnp.float32
BF16 = jnp.bfloat16
I16 = jnp.int16
I32 = jnp.int32

EPS = 1e-6
ROPE_THETA = 10000.0
D_MODEL = 1024
MLSTM_HEADS = 4
MLSTM_HD = 256
CONV_W = 4
DSA_HEADS = 8
DSA_HD = 128
IDX_HEADS = 8
IDX_HD = 64
DSA_TOPK = 256
VIS_CHUNK = 64
MEM_HEADS = 4
MEM_HD = 256
N_BRANCH = 3

LANES = 128
SUBLANES = 8
PACKED_ROWS = 16
SCAN_CHUNK = 256
DSA_TILE = 256
SCORE_ROWS = 64
ROW_TILE = 256
NEG_BIG = -1e30
INT_MIN = -2 ** 31
LOG2E = math.log2(math.e)
VMEM_LIMIT = 56 * 1024 * 1024


def _sigmoid(x):
    return 1.0 / (1.0 + jnp.exp(-x))


def _silu(x):
    return x * _sigmoid(x)


def _log_sigmoid(x):
    return jnp.minimum(x, 0.0) - jnp.log(1.0 + jnp.exp(-jnp.abs(x)))


def _dot(a, b):
    return jnp.dot(a, b, preferred_element_type=F32)


def _dot_nt(a, b):
    return lax.dot_general(a, b, (((1,), (1,)), ((), ())), preferred_element_type=F32)


def _dot_tn(a, b):
    return lax.dot_general(a, b, (((0,), (0,)), ((), ())), preferred_element_type=F32)


def _rms_rows(x, g):
    ms = jnp.mean(x * x, axis=-1, keepdims=True)
    return x * lax.rsqrt(ms + EPS) * g


def _normed_input(x_ref, g_ref):
    return _rms_rows(x_ref[...], g_ref[...]).astype(BF16)


def _rotate(x, cos, sin_signed):
    return x * cos + pltpu.roll(x, LANES // 2, 1) * sin_signed


def _proj_a_kernel(x_ref, g_ref, w_ref, wif_ref, qk_ref, v_ref, og_ref, if_ref):
    d = MLSTM_HEADS * MLSTM_HD
    h = _normed_input(x_ref, g_ref)
    qk_ref[...] = _dot(h, w_ref[:, 0:2 * d])
    v_ref[...] = _dot(h, w_ref[:, 2 * d:3 * d]).astype(BF16)
    o = _dot(h, w_ref[:, 3 * d:4 * d])
    z = _dot(h, w_ref[:, 4 * d:5 * d])
    og_ref[...] = _sigmoid(o) * _silu(z)
    if_ref[...] = _dot(h, wif_ref[...])


def _proj_a(x2, g, w_a, w_if):
    t = x2.shape[0]
    d = MLSTM_HEADS * MLSTM_HD
    tm = ROW_TILE
    row = lambda n: pl.BlockSpec((tm, n), lambda i: (i, 0))
    full = lambda a: pl.BlockSpec(a.shape, lambda i: (0, 0))
    return pl.pallas_call(
        _proj_a_kernel,
        grid=(t // tm,),
        in_specs=[row(D_MODEL), full(g), full(w_a), full(w_if)],
        out_specs=[row(2 * d), row(d), row(d), row(LANES)],
        out_shape=[jax.ShapeDtypeStruct((t, 2 * d), F32), jax.ShapeDtypeStruct((t, d), BF16),
                   jax.ShapeDtypeStruct((t, d), F32), jax.ShapeDtypeStruct((t, LANES), F32)],
        compiler_params=pltpu.CompilerParams(dimension_semantics=("arbitrary",),
                                             vmem_limit_bytes=VMEM_LIMIT),
        name="proj_a",
    )(x2, g, w_a, w_if)


def _mlstm_kernel(qk_ref, v_ref, og_ref, ifc_ref, ifr_ref, cw_ref, cb_ref, gbc_ref, gbr_ref, ng_ref,
                  y_ref, ext_ref, c_ref, n_ref, m_ref):
    L = SCAN_CHUNK
    dh = MLSTM_HD
    d = MLSTM_HEADS * dh
    halo = 8

    @pl.when(pl.program_id(1) == 0)
    def _():
        ext_ref[0:halo, :] = jnp.zeros((halo, 2 * d), F32)
        c_ref[...] = jnp.zeros_like(c_ref)
        n_ref[...] = jnp.zeros_like(n_ref)
        m_ref[...] = jnp.zeros_like(m_ref)

    ext_ref[halo:halo + L, :] = qk_ref[...]

    def conv_silu(c0):
        u = cb_ref[:, c0:c0 + dh]
        for j in range(CONV_W):
            r0 = halo - (CONV_W - 1) + j
            u = u + cw_ref[j:j + 1, c0:c0 + dh] * ext_ref[r0:r0 + L, c0:c0 + dh]
        return _silu(u)

    row_i = lax.broadcasted_iota(I32, (L, L), 0)
    col_i = lax.broadcasted_iota(I32, (L, L), 1)
    causal = row_i >= col_i
    tril = jnp.where(causal, 1.0, 0.0).astype(F32)
    triu = jnp.where(col_i >= row_i, 1.0, 0.0).astype(F32)

    ifc = ifc_ref[...] + gbc_ref[...]
    ifr = ifr_ref[...] + gbr_ref[...]
    bcol_all = jnp.dot(tril, _log_sigmoid(ifc), preferred_element_type=F32,
                       precision=lax.Precision.HIGHEST)
    brow_all = jnp.dot(_log_sigmoid(ifr), triu, preferred_element_type=F32,
                       precision=lax.Precision.HIGHEST)

    for h in range(MLSTM_HEADS):
        qf = conv_silu(h * dh)
        kf = conv_silu(d + h * dh) * (dh ** -0.5)
        q = qf.astype(BF16)
        v = v_ref[:, h * dh:(h + 1) * dh]
        icol = ifc[:, h:h + 1]
        irow = ifr[h:h + 1, :]
        bcol = bcol_all[:, MLSTM_HEADS + h:MLSTM_HEADS + h + 1]
        brow = brow_all[MLSTM_HEADS + h:MLSTM_HEADS + h + 1, :]
        m_prev = m_ref[h:h + 1, 0:1]

        d_log = jnp.where(causal, bcol - brow + irow, -jnp.inf)
        inter = bcol + m_prev
        m_t = jnp.maximum(inter, jnp.max(d_log, axis=-1, keepdims=True))
        w_intra = jnp.exp(d_log - m_t)
        w_inter = jnp.exp(inter - m_t)
        a = _dot_nt(q, kf.astype(BF16)) * w_intra
        nrow = n_ref[h]
        num = w_inter * _dot(q, c_ref[h].astype(BF16)) + _dot(a.astype(BF16), v)
        den = w_inter * jnp.sum(qf * nrow, axis=-1, keepdims=True) + jnp.sum(a, axis=-1, keepdims=True)
        hh = num / jnp.maximum(jnp.abs(den), jnp.exp(-m_t))
        hn = _rms_rows(hh, ng_ref[h])
        y_ref[:, h * dh:(h + 1) * dh] = (hn * og_ref[:, h * dh:(h + 1) * dh]).astype(BF16)

        b_last = bcol[L - 1:L, :]
        g_col = b_last - bcol + icol
        m_new = jnp.maximum(b_last + m_prev, jnp.max(g_col, axis=0, keepdims=True))
        decay = jnp.exp(b_last + m_prev - m_new)
        kw = kf * jnp.exp(g_col - m_new)
        c_ref[h] = decay * c_ref[h] + _dot_tn(kw.astype(BF16), v)
        n_ref[h] = decay * nrow + jnp.sum(kw, axis=0, keepdims=True)
        m_ref[h:h + 1, :] = jnp.broadcast_to(m_new, (1, LANES))

    ext_ref[0:halo, :] = ext_ref[L:L + halo, :]


def _mlstm(qk, v, og, ifc, ifr, conv_w, conv_b, gate_col, gate_row, norm_g, batch):
    t = qk.shape[0]
    L = SCAN_CHUNK
    d = MLSTM_HEADS * MLSTM_HD
    nc = t // batch // L
    row = lambda n: pl.BlockSpec((L, n), lambda b, c: (b * nc + c, 0))
    full = lambda a: pl.BlockSpec(a.shape, lambda b, c: (0,) * a.ndim)
    return pl.pallas_call(
        _mlstm_kernel,
        grid=(batch, nc),
        in_specs=[row(2 * d), row(d), row(d), row(LANES),
                  pl.BlockSpec((None, 8, L), lambda b, c: (b, 0, c)),
                  full(conv_w), full(conv_b), full(gate_col), full(gate_row), full(norm_g)],
        out_specs=row(d),
        out_shape=jax.ShapeDtypeStruct((t, d), BF16),
        scratch_shapes=[pltpu.VMEM((L + 8, 2 * d), F32),
                        pltpu.VMEM((MLSTM_HEADS, MLSTM_HD, MLSTM_HD), F32),
                        pltpu.VMEM((MLSTM_HEADS, 1, MLSTM_HD), F32),
                        pltpu.VMEM((8, LANES), F32)],
        compiler_params=pltpu.CompilerParams(dimension_semantics=("arbitrary", "arbitrary"),
                                             vmem_limit_bytes=VMEM_LIMIT),
        name="mlstm",
    )(qk, v, og, ifc, ifr, conv_w, conv_b, gate_col, gate_row, norm_g)


def _proj_b_kernel(x_ref, g_ref, w_ref, cosd_ref, sind_ref, gq_ref, gk_ref,
                   qt_ref, k_ref, vt_ref, z_ref, qit_ref, ki_ref, wit_ref):
    d = DSA_HEADS * DSA_HD
    h = _normed_input(x_ref, g_ref)
    cosd = cosd_ref[...]
    sind = sind_ref[...]
    cosi, sini = cosd, sind

    acc = _dot(h, w_ref[:, 0:d])
    for hd in range(DSA_HEADS):
        sl = slice(hd * DSA_HD, (hd + 1) * DSA_HD)
        xs = _rotate(_rms_rows(acc[:, sl], gq_ref[...]), cosd, sind) * (DSA_HD ** -0.5 * LOG2E)
        qt_ref[sl, :] = xs.T.astype(BF16)
    acc = _dot(h, w_ref[:, d:2 * d])
    for hd in range(DSA_HEADS):
        sl = slice(hd * DSA_HD, (hd + 1) * DSA_HD)
        k_ref[:, sl] = _rotate(_rms_rows(acc[:, sl], gk_ref[...]), cosd, sind).astype(BF16)
    acc = _dot(h, w_ref[:, 2 * d:3 * d])
    for hd in range(DSA_HEADS):
        sl = slice(hd * DSA_HD, (hd + 1) * DSA_HD)
        vt_ref[sl, :] = acc[:, sl].T.astype(BF16)
    z_ref[...] = _silu(_dot(h, w_ref[:, 3 * d:4 * d]))
    acc = _dot(h, w_ref[:, 4 * d:4 * d + IDX_HEADS * LANES])
    for hd in range(IDX_HEADS):
        sl = slice(hd * LANES, (hd + 1) * LANES)
        qit_ref[sl, :] = (_rotate(acc[:, sl], cosi, sini) * (IDX_HD ** -0.5)).T.astype(BF16)
    c0 = 4 * d + IDX_HEADS * LANES
    ki_ref[...] = _rotate(_dot(h, w_ref[:, c0:c0 + LANES]), cosi, sini).astype(BF16)
    wi = _dot(h, w_ref[:, c0 + LANES:c0 + 2 * LANES]) * (IDX_HEADS ** -0.5)
    wit_ref[...] = wi.T[0:IDX_HEADS, :]


def _proj_b(x2, g, w_b, cosd, sind, gq, gk):
    t = x2.shape[0]
    d = DSA_HEADS * DSA_HD
    tm = DSA_TILE
    nt = t // tm
    row = lambda n: pl.BlockSpec((tm, n), lambda i: (i, 0))
    full = lambda a: pl.BlockSpec(a.shape, lambda i: (0, 0))
    fmaj = lambda n: pl.BlockSpec((None, n, tm), lambda i: (i, 0, 0))
    sds = jax.ShapeDtypeStruct
    return pl.pallas_call(
        _proj_b_kernel,
        grid=(nt,),
        in_specs=[row(D_MODEL), full(g), full(w_b), row(LANES), row(LANES), full(gq), full(gk)],
        out_specs=[fmaj(d), row(d), fmaj(d), row(d), fmaj(IDX_HEADS * LANES), row(LANES), fmaj(IDX_HEADS)],
        out_shape=[sds((nt, d, tm), BF16), sds((t, d), BF16), sds((nt, d, tm), BF16), sds((t, d), F32),
                   sds((nt, IDX_HEADS * LANES, tm), BF16), sds((t, LANES), BF16),
                   sds((nt, IDX_HEADS, tm), F32)],
        compiler_params=pltpu.CompilerParams(dimension_semantics=("arbitrary",),
                                             vmem_limit_bytes=VMEM_LIMIT),
        name="proj_b",
    )(x2, g, w_b, cosd, sind, gq, gk)


def _sortable_key(x):
    bits = pltpu.bitcast(x, I32)
    return bits ^ ((bits >> 31) & jnp.int32(0x7FFFFFFF))


def _fold_rows(x, rows, op=jnp.add):
    acc = x[0:rows, :]
    for r in range(1, x.shape[0] // rows):
        acc = op(acc, x[r * rows:(r + 1) * rows, :])
    return acc


def _reduce_rows8(x8, op):
    for shift in (4, 2, 1):
        x8 = op(x8, pltpu.roll(x8, shift, 0))
    return x8[0:1, :]


def _dsa_kernel(qt_ref, qit_ref, wit_ref, zb_ref, ki_ref, k_hbm, vt_hbm, o_ref,
                kbuf, vbuf, key_ref, k16_ref, gmax_ref, acc_ref, m_ref, l_ref, s_ref, mt_ref, sem,
                *, top_k):
    T = DSA_TILE
    C = SCORE_ROWS
    b = pl.program_id(0)
    j = pl.program_id(1)

    def kv_copies():
        return (pltpu.make_async_copy(k_hbm.at[b], kbuf, sem.at[0]),
                pltpu.make_async_copy(vt_hbm.at[b], vbuf, sem.at[1]))

    @pl.when(j == 0)
    def _():
        for cp in kv_copies():
            cp.start()

    nt = j + 1
    t_q = j * T + lax.broadcasted_iota(I32, (1, T), 1)
    vis_end = (t_q // VIS_CHUNK + 1) * VIS_CHUNK
    row_pos = lax.broadcasted_iota(I32, (T, T), 0)
    row_pos_c = lax.broadcasted_iota(I32, (C, T), 0)

    gmax_ref[...] = jnp.full((T, T), INT_MIN, I32)

    def score_body(t, carry):
        for c in range(T // C):
            rows = slice(c * C, (c + 1) * C)
            sc = jnp.zeros((C, T), F32)
            for h in range(IDX_HEADS):
                lg = _dot(ki_ref[t, rows, :], qit_ref[h * LANES:(h + 1) * LANES, :])
                sc = sc + jnp.maximum(lg, 0.0) * wit_ref[h:h + 1, :]
            key = jnp.where(t * T + c * C + row_pos_c < vis_end, _sortable_key(sc), INT_MIN)
            key_ref[t, rows, :] = key
            k16_ref[t, rows, :] = (key >> 16).astype(I16)
            gmax_ref[rows, :] = jnp.maximum(gmax_ref[rows, :], key)
        return carry

    lax.fori_loop(0, nt, score_body, 0)

    def pad_k16():
        @pl.when(nt % 2 == 1)
        def _():
            k16_ref[nt] = jnp.full((T, T), -32768, I16)

    pad_k16()

    def count16_ge(cand_row):
        cand = jnp.broadcast_to(cand_row.astype(I16), (PACKED_ROWS, T))
        n_acc = 4

        def body(pair, accs):
            accs = list(accs)
            for u in range(2):
                for r in range(T // PACKED_ROWS):
                    blk = k16_ref[2 * pair + u, r * PACKED_ROWS:(r + 1) * PACKED_ROWS, :]
                    accs[r % n_acc] = accs[r % n_acc] + jnp.where(blk >= cand, jnp.int16(1), jnp.int16(0))
            return tuple(accs)

        zero = jnp.zeros((PACKED_ROWS, T), I16)
        accs = lax.fori_loop(0, (nt + 1) // 2, body, (zero,) * n_acc)
        acc = (accs[0] + accs[1]) + (accs[2] + accs[3])
        return jnp.sum(acc.astype(I32), axis=0, keepdims=True)

    gm = gmax_ref[...]
    lo0 = _reduce_rows8(_fold_rows(gm, SUBLANES, jnp.minimum), jnp.minimum) >> 16
    hi0 = _reduce_rows8(_fold_rows(gm, SUBLANES, jnp.maximum), jnp.maximum) >> 16
    span = jnp.max((hi0 - lo0).astype(F32)).astype(I32)
    n_iter = jnp.int32(0)
    for bit in range(16):
        n_iter = n_iter + jnp.where((span >> bit) > 0, 1, 0)

    def hi_body(i, carry):
        lo, hi = carry
        mid = lo + ((hi - lo + 1) >> 1)
        ok = count16_ge(mid) >= top_k
        return jnp.where(ok, mid, lo), jnp.where(ok, hi, mid - 1)

    thr_hi, _ = lax.fori_loop(0, n_iter, hi_body, (lo0, hi0))
    n_above = count16_ge(thr_hi + 1) * jnp.where(thr_hi == 32767, 0, 1)

    def low_body(t, carry):
        key = key_ref[t]
        low = (key & 0xFFFF) - 32768
        k16_ref[t] = jnp.where((key >> 16) == thr_hi, low, -32768).astype(I16)
        return carry

    lax.fori_loop(0, nt, low_body, 0)
    need_lo = top_k - n_above

    def lo_body(i, thr_u):
        cand_u = thr_u | lax.shift_left(jnp.int32(1), 15 - i)
        return jnp.where(count16_ge(cand_u - 32768) >= need_lo, cand_u, thr_u)

    thr_lo = lax.fori_loop(0, 16, lo_body, jnp.zeros((1, T), I32)) - 32768
    thr = thr_hi * 65536 + (thr_lo + 32768)

    def count32_body(t, carry):
        n_ge, n_gt = carry
        key = key_ref[t]
        n_ge = n_ge + _fold_rows(jnp.where(key >= thr, 1.0, 0.0), SUBLANES)
        n_gt = n_gt + _fold_rows(jnp.where(key > thr, 1.0, 0.0), SUBLANES)
        return n_ge, n_gt

    zero8 = jnp.zeros((SUBLANES, T), F32)
    n_ge, n_gt = lax.fori_loop(0, nt, count32_body, (zero8, zero8))
    n_ge = jnp.sum(n_ge, axis=0, keepdims=True)
    n_gt = jnp.sum(n_gt, axis=0, keepdims=True)
    need = float(top_k) - n_gt
    surplus = jnp.max(n_ge - n_gt - need)

    @pl.when(surplus > 0.0)
    def _():
        c_i = lax.broadcasted_iota(I32, (T, T), 1)
        prefix = jnp.where(c_i <= row_pos, 1.0, 0.0).astype(BF16)
        demoted = jnp.where(thr == INT_MIN, INT_MIN, thr - 1)

        def tie_body(t, seen):
            key = key_ref[t]
            eq = key == thr
            eqf = jnp.where(eq, 1.0, 0.0)
            incl = _dot(prefix, eqf.astype(BF16))
            rank = seen + incl - eqf
            key_ref[t] = jnp.where(eq & (rank >= need), demoted, key)
            return seen + incl[T - 1:T, :]

        lax.fori_loop(0, nt, tie_body, jnp.zeros((1, T), F32))

    def bias_body(t, carry):
        key = key_ref[t]
        keep = jnp.where(t * T + row_pos < vis_end, jnp.where(key >= thr, 0.0, NEG_BIG), NEG_BIG)
        key_ref[t] = pltpu.bitcast(keep.astype(F32), I32)
        return carry

    lax.fori_loop(0, nt, bias_body, 0)

    m_ref[...] = jnp.full(m_ref.shape, NEG_BIG, F32)
    l_ref[...] = jnp.zeros_like(l_ref)
    acc_ref[...] = jnp.zeros_like(acc_ref)

    @pl.when(j == 0)
    def _():
        for cp in kv_copies():
            cp.wait()

    def logits(t, h):
        sl = slice(h * DSA_HD, (h + 1) * DSA_HD)
        s = _dot(kbuf[t, :, sl], qt_ref[sl, :]) + pltpu.bitcast(key_ref[t], F32)
        s_ref[h] = s
        mt_ref[h] = _fold_rows(s, SUBLANES, jnp.maximum)

    def accumulate(t, h):
        sl = slice(h * DSA_HD, (h + 1) * DSA_HD)
        m_old = m_ref[h]
        m_new = jnp.maximum(m_old, jnp.max(mt_ref[h], axis=0, keepdims=True))
        alpha = jnp.exp2(m_old - m_new)
        p = jnp.exp2(s_ref[h] - m_new)
        l_ref[h] = alpha * l_ref[h] + _fold_rows(p, SUBLANES)
        acc_ref[sl, :] = alpha * acc_ref[sl, :] + _dot(vbuf[t, sl, :], p.astype(BF16))
        m_ref[h] = m_new

    for h in range(DSA_HEADS):
        logits(0, h)

    def attn_body(t, carry):
        for h in range(DSA_HEADS):
            accumulate(t, h)
            logits(t + 1, h)
        return carry

    lax.fori_loop(0, nt - 1, attn_body, 0)
    for h in range(DSA_HEADS):
        accumulate(nt - 1, h)

    for h in range(DSA_HEADS):
        sl = slice(h * DSA_HD, (h + 1) * DSA_HD)
        out_t = acc_ref[sl, :] / jnp.sum(l_ref[h], axis=0, keepdims=True)
        o_ref[:, sl] = (out_t.T * zb_ref[:, sl]).astype(BF16)


def _dsa(qt, k, vt, zb, qit, ki, wit, batch):
    t, d = k.shape
    s = t // batch
    T = DSA_TILE
    nb = s // T
    top_k = min(DSA_TOPK, s // 4)
    assert top_k <= T and nb % 2 == 0
    k4 = k.reshape(batch, nb, T, d)
    vt4 = vt.reshape(batch, nb, d, T)
    ki4 = ki.reshape(batch, nb, T, LANES)
    fmaj = lambda n: pl.BlockSpec((None, n, T), lambda b, j: (b * nb + j, 0, 0))
    row = lambda n: pl.BlockSpec((T, n), lambda b, j: (b * nb + j, 0))
    return pl.pallas_call(
        functools.partial(_dsa_kernel, top_k=top_k),
        grid=(batch, nb),
        in_specs=[fmaj(d), fmaj(IDX_HEADS * LANES), fmaj(IDX_HEADS), row(d),
                  pl.BlockSpec((None, nb, T, LANES), lambda b, j: (b, 0, 0, 0), pipeline_mode=pl.Buffered(1)),
                  pl.BlockSpec(memory_space=pl.ANY), pl.BlockSpec(memory_space=pl.ANY)],
        out_specs=row(d),
        out_shape=jax.ShapeDtypeStruct((t, d), BF16),
        scratch_shapes=[pltpu.VMEM((nb, T, d), BF16), pltpu.VMEM((nb, d, T), BF16),
                        pltpu.VMEM((nb, T, T), I32), pltpu.VMEM((nb, T, T), I16),
                        pltpu.VMEM((T, T), I32),
                        pltpu.VMEM((d, T), F32), pltpu.VMEM((DSA_HEADS, 1, T), F32),
                        pltpu.VMEM((DSA_HEADS, SUBLANES, T), F32),
                        pltpu.VMEM((DSA_HEADS, T, T), F32),
                        pltpu.VMEM((DSA_HEADS, SUBLANES, T), F32),
                        pltpu.SemaphoreType.DMA((2,))],
        compiler_params=pltpu.CompilerParams(dimension_semantics=("arbitrary", "arbitrary"),
                                             vmem_limit_bytes=VMEM_LIMIT),
        name="dsa",
    )(qt, qit, wit, zb, ki4, k4, vt4)


def _mem_kv_kernel(mem_ref, g_ref, w_ref, gk_ref, km_ref, vm_ref):
    d = MEM_HEADS * MEM_HD
    mh = _rms_rows(mem_ref[...], g_ref[...]).astype(BF16)
    kv = _dot(mh, w_ref[...])
    for h in range(MEM_HEADS):
        sl = slice(h * MEM_HD, (h + 1) * MEM_HD)
        km_ref[:, sl] = _rms_rows(kv[:, sl], gk_ref[...]).astype(BF16)
    vm_ref[...] = kv[:, d:2 * d].astype(BF16)


def _mem_kv(mem, g, w, gk):
    batch, nm, _ = mem.shape
    d = MEM_HEADS * MEM_HD
    full = lambda a: pl.BlockSpec(a.shape, lambda b: (0, 0))
    blk = lambda n: pl.BlockSpec((None, nm, n), lambda b: (b, 0, 0))
    return pl.pallas_call(
        _mem_kv_kernel,
        grid=(batch,),
        in_specs=[blk(D_MODEL), full(g), full(w), full(gk)],
        out_specs=[blk(d), blk(d)],
        out_shape=[jax.ShapeDtypeStruct((batch, nm, d), BF16)] * 2,
        compiler_params=pltpu.CompilerParams(dimension_semantics=("arbitrary",),
                                             vmem_limit_bytes=VMEM_LIMIT),
        name="mem_kv",
    )(mem, g, w, gk)


def _mem_attn_kernel(x_ref, g_ref, w_ref, gq_ref, km_ref, vm_ref, y_ref):
    d = MEM_HEADS * MEM_HD
    h = _normed_input(x_ref, g_ref)
    qm = _dot(h, w_ref[:, 0:d])
    zm = _dot(h, w_ref[:, d:2 * d])
    for hd in range(MEM_HEADS):
        sl = slice(hd * MEM_HD, (hd + 1) * MEM_HD)
        qs = (_rms_rows(qm[:, sl], gq_ref[...]) * (MEM_HD ** -0.5)).astype(BF16)
        s = _dot_nt(qs, km_ref[:, sl])
        p = jnp.exp(s - jnp.max(s, axis=-1, keepdims=True))
        o = _dot(p.astype(BF16), vm_ref[:, sl]) / jnp.sum(p, axis=-1, keepdims=True)
        y_ref[:, sl] = (o * _silu(zm[:, sl])).astype(BF16)


def _mem_attn(x2, g, w_m, gq, km, vm):
    t = x2.shape[0]
    batch, nm, d = km.shape
    tm = ROW_TILE
    per_b = t // batch // tm
    row = lambda n: pl.BlockSpec((tm, n), lambda i: (i, 0))
    full = lambda a: pl.BlockSpec(a.shape, lambda i: (0, 0))
    mem = pl.BlockSpec((None, nm, d), lambda i: (i // per_b, 0, 0))
    return pl.pallas_call(
        _mem_attn_kernel,
        grid=(t // tm,),
        in_specs=[row(D_MODEL), full(g), full(w_m), full(gq), mem, mem],
        out_specs=row(d),
        out_shape=jax.ShapeDtypeStruct((t, d), BF16),
        compiler_params=pltpu.CompilerParams(dimension_semantics=("arbitrary",),
                                             vmem_limit_bytes=VMEM_LIMIT),
        name="mem_attn",
    )(x2, g, w_m, gq, km, vm)


def _merge_kernel(x_ref, g_ref, wg_ref, ya_ref, yb_ref, ym_ref, wb_ref, wo_ref, o_ref):
    x = x_ref[...]
    h = _rms_rows(x, g_ref[...]).astype(BF16)
    merged = None
    for i, y_ref in enumerate((ya_ref, yb_ref, ym_ref)):
        gate = _sigmoid(_dot(h, wg_ref[:, i * D_MODEL:(i + 1) * D_MODEL]))
        term = gate * _dot(y_ref[...], wb_ref[i])
        merged = term if merged is None else merged + term
    o_ref[...] = x + _dot(merged.astype(BF16), wo_ref[...])


def _merge(x2, g, w_gate, ya, yb, ym, w_branch, w_out):
    t = x2.shape[0]
    tm = ROW_TILE
    row = lambda n: pl.BlockSpec((tm, n), lambda i: (i, 0))
    full = lambda a: pl.BlockSpec(a.shape, lambda i: (0,) * a.ndim)
    return pl.pallas_call(
        _merge_kernel,
        grid=(t // tm,),
        in_specs=[row(D_MODEL), full(g), full(w_gate), row(D_MODEL), row(D_MODEL), row(D_MODEL),
                  full(w_branch), full(w_out)],
        out_specs=row(D_MODEL),
        out_shape=jax.ShapeDtypeStruct((t, D_MODEL), F32),
        compiler_params=pltpu.CompilerParams(dimension_semantics=("arbitrary",),
                                             vmem_limit_bytes=VMEM_LIMIT),
        name="merge",
    )(x2, g, w_gate, ya, yb, ym, w_branch, w_out)


def _rope_tables(positions):
    half = DSA_HD // 2
    inv_freq = jnp.power(ROPE_THETA, -jnp.arange(half, dtype=F32) / half)
    ang = positions.astype(F32)[..., None] * inv_freq
    cos, sin = jnp.cos(ang), jnp.sin(ang)
    cos_t = jnp.concatenate([cos, cos], axis=-1)
    sin_t = jnp.concatenate([-sin, sin], axis=-1)
    return cos_t.reshape(-1, LANES), sin_t.reshape(-1, LANES)


def _pad_idx_cols(w, heads):
    dm = w.shape[0]
    half = IDX_HD // 2
    w5 = w.reshape(dm, heads, 2, half, 1)
    w5 = jnp.concatenate([w5, jnp.zeros_like(w5)], axis=-1)
    return w5.reshape(dm, heads * LANES)


def _pad_cols(w, n):
    return jnp.pad(w, ((0, 0), (0, n - w.shape[1])))


def _layer(x, mem, positions, norm_g, mem_norm_g, w_in, conv_w, conv_b, gate_b, mlstm_norm_g,
           dsa_q_norm_g, dsa_k_norm_g, mem_q_norm_g, mem_k_norm_g, w_mem_kv, w_branch, w_out):
    batch, seq, dm = x.shape
    x2 = x.reshape(batch * seq, dm)
    d = D_MODEL
    splits = (2 * d, d, 2 * MLSTM_HEADS, d, d, d, d, d, d, IDX_HEADS * IDX_HD, IDX_HD, IDX_HEADS,
              d, d, N_BRANCH * d)
    cuts = np.cumsum(np.array(splits))[:-1].tolist()
    (w_qk, w_va, w_if, w_oa, w_za, w_qb, w_kb, w_vb, w_zb, w_qi, w_ki, w_wi,
     w_qm, w_zm, w_gate) = jnp.split(w_in, cuts, axis=-1)
    g = norm_g.reshape(1, dm)

    w_a = jnp.concatenate([w_qk, w_va, w_oa, w_za], axis=1).astype(BF16)
    qk, va, og, ifc = _proj_a(x2, g, w_a, _pad_cols(w_if, LANES).astype(BF16))
    ifr = ifc[:, :8].reshape(batch, seq, 8).transpose(0, 2, 1)
    gb = gate_b.reshape(2 * MLSTM_HEADS)
    ya = _mlstm(qk, va, og, ifc, ifr, conv_w, conv_b.reshape(1, -1),
                _pad_cols(gb.reshape(1, -1), LANES), gb.reshape(-1, 1),
                mlstm_norm_g.reshape(MLSTM_HEADS, 1, MLSTM_HD), batch)

    w_b = jnp.concatenate([w_qb, w_kb, w_vb, w_zb, _pad_idx_cols(w_qi, IDX_HEADS),
                           _pad_idx_cols(w_ki, 1), _pad_cols(w_wi, LANES)], axis=1).astype(BF16)
    cosd, sind = _rope_tables(positions)
    qt, kb, vt, zb, qit, ki, wit = _proj_b(x2, g, w_b, cosd, sind,
                                           dsa_q_norm_g.reshape(1, -1), dsa_k_norm_g.reshape(1, -1))
    yb = _dsa(qt, kb, vt, zb, qit, ki, wit, batch)

    km, vm = _mem_kv(mem, mem_norm_g.reshape(1, dm), w_mem_kv.astype(BF16), mem_k_norm_g.reshape(1, -1))
    ym = _mem_attn(x2, g, jnp.concatenate([w_qm, w_zm], axis=1).astype(BF16),
                   mem_q_norm_g.reshape(1, -1), km, vm)

    out = _merge(x2, g, w_gate.astype(BF16), ya, yb, ym, w_branch.astype(BF16), w_out.astype(BF16))
    return out.reshape(batch, seq, dm)


def kernel(x, mem, positions, norm_g, mem_norm_g, w_in, conv_w, conv_b, mlstm_gate_b, mlstm_norm_g,
           dsa_q_norm_g, dsa_k_norm_g, mem_q_norm_g, mem_k_norm_g, w_mem_kv, w_branch, w_out):
    for l in range(norm_g.shape[0]):
        x = _layer(x, mem, positions, norm_g[l], mem_norm_g[l], w_in[l], conv_w[l], conv_b[l],
                   mlstm_gate_b[l], mlstm_norm_g[l], dsa_q_norm_g[l], dsa_k_norm_g[l],
                   mem_q_norm_g[l], mem_k_norm_g[l], w_mem_kv[l], w_branch[l], w_out[l])
    return x
```

```python
import functools
import math

import numpy as np
import jax
import jax.numpy as jnp
from jax import lax
from jax.experimental import pallas as pl
from jax.experimental.pallas import tpu as pltpu

F32 = jnp.float32
BF16 = jnp.bfloat16
I16 = jnp.int16
I32 = jnp.int32

EPS = 1e-6
ROPE_THETA = 10000.0
D_MODEL = 1024
MLSTM_HEADS = 4
MLSTM_HD = 256
CONV_W = 4
DSA_HEADS = 8
DSA_HD = 128
IDX_HEADS = 8
IDX_HD = 64
DSA_TOPK = 256
VIS_CHUNK = 64
MEM_HEADS = 4
MEM_HD = 256
N_BRANCH = 3

LANES = 128
SUBLANES = 8
PACKED_ROWS = 16
SCAN_CHUNK = 256
DSA_TILE = 256
SCORE_ROWS = 64
ROW_TILE = 256
NEG_BIG = -1e30
INT_MIN = -2 ** 31
LOG2E = math.log2(math.e)
VMEM_LIMIT = 56 * 1024 * 1024


def _sigmoid(x):
    return 1.0 / (1.0 + jnp.exp(-x))


def _silu(x):
    return x * _sigmoid(x)


def _log_sigmoid(x):
    return jnp.minimum(x, 0.0) - jnp.log(1.0 + jnp.exp(-jnp.abs(x)))


def _dot(a, b):
    return jnp.dot(a, b, preferred_element_type=F32)


def _dot_nt(a, b):
    return lax.dot_general(a, b, (((1,), (1,)), ((), ())), preferred_element_type=F32)


def _dot_tn(a, b):
    return lax.dot_general(a, b, (((0,), (0,)), ((), ())), preferred_element_type=F32)


def _rms_rows(x, g):
    ms = jnp.mean(x * x, axis=-1, keepdims=True)
    return x * lax.rsqrt(ms + EPS) * g


def _normed_input(x_ref, g_ref):
    return _rms_rows(x_ref[...], g_ref[...]).astype(BF16)


def _rotate(x, cos, sin_signed):
    return x * cos + pltpu.roll(x, LANES // 2, 1) * sin_signed


def _proj_a_kernel(x_ref, g_ref, w_ref, wif_ref, qk_ref, v_ref, og_ref, if_ref):
    d = MLSTM_HEADS * MLSTM_HD
    h = _normed_input(x_ref, g_ref)
    qk_ref[...] = _dot(h, w_ref[:, 0:2 * d])
    v_ref[...] = _dot(h, w_ref[:, 2 * d:3 * d]).astype(BF16)
    o = _dot(h, w_ref[:, 3 * d:4 * d])
    z = _dot(h, w_ref[:, 4 * d:5 * d])
    og_ref[...] = _sigmoid(o) * _silu(z)
    if_ref[...] = _dot(h, wif_ref[...])


def _proj_a(x2, g, w_a, w_if):
    t = x2.shape[0]
    d = MLSTM_HEADS * MLSTM_HD
    tm = ROW_TILE
    row = lambda n: pl.BlockSpec((tm, n), lambda i: (i, 0))
    full = lambda a: pl.BlockSpec(a.shape, lambda i: (0, 0))
    return pl.pallas_call(
        _proj_a_kernel,
        grid=(t // tm,),
        in_specs=[row(D_MODEL), full(g), full(w_a), full(w_if)],
        out_specs=[row(2 * d), row(d), row(d), row(LANES)],
        out_shape=[jax.ShapeDtypeStruct((t, 2 * d), F32), jax.ShapeDtypeStruct((t, d), BF16),
                   jax.ShapeDtypeStruct((t, d), F32), jax.ShapeDtypeStruct((t, LANES), F32)],
        compiler_params=pltpu.CompilerParams(dimension_semantics=("arbitrary",),
                                             vmem_limit_bytes=VMEM_LIMIT),
        name="proj_a",
    )(x2, g, w_a, w_if)


def _mlstm_kernel(qk_ref, v_ref, og_ref, ifc_ref, ifr_ref, cw_ref, cb_ref, gbc_ref, gbr_ref, ng_ref,
                  y_ref, ext_ref, c_ref, n_ref, m_ref):
    L = SCAN_CHUNK
    dh = MLSTM_HD
    d = MLSTM_HEADS * dh
    halo = 8

    @pl.when(pl.program_id(1) == 0)
    def _():
        ext_ref[0:halo, :] = jnp.zeros((halo, 2 * d), F32)
        c_ref[...] = jnp.zeros_like(c_ref)
        n_ref[...] = jnp.zeros_like(n_ref)
        m_ref[...] = jnp.zeros_like(m_ref)

    ext_ref[halo:halo + L, :] = qk_ref[...]

    def conv_silu(c0):
        u = cb_ref[:, c0:c0 + dh]
        for j in range(CONV_W):
            r0 = halo - (CONV_W - 1) + j
            u = u + cw_ref[j:j + 1, c0:c0 + dh] * ext_ref[r0:r0 + L, c0:c0 + dh]
        return _silu(u)

    row_i = lax.broadcasted_iota(I32, (L, L), 0)
    col_i = lax.broadcasted_iota(I32, (L, L), 1)
    causal = row_i >= col_i
    tril = jnp.where(causal, 1.0, 0.0).astype(F32)
    triu = jnp.where(col_i >= row_i, 1.0, 0.0).astype(F32)

    ifc = ifc_ref[...] + gbc_ref[...]
    ifr = ifr_ref[...] + gbr_ref[...]
    bcol_all = jnp.dot(tril, _log_sigmoid(ifc), preferred_element_type=F32,
                       precision=lax.Precision.HIGHEST)
    brow_all = jnp.dot(_log_sigmoid(ifr), triu, preferred_element_type=F32,
                       precision=lax.Precision.HIGHEST)

    for h in range(MLSTM_HEADS):
        qf = conv_silu(h * dh)
        kf = conv_silu(d + h * dh) * (dh ** -0.5)
        q = qf.astype(BF16)
        v = v_ref[:, h * dh:(h + 1) * dh]
        icol = ifc[:, h:h + 1]
        irow = ifr[h:h + 1, :]
        bcol = bcol_all[:, MLSTM_HEADS + h:MLSTM_HEADS + h + 1]
        brow = brow_all[MLSTM_HEADS + h:MLSTM_HEADS + h + 1, :]
        m_prev = m_ref[h:h + 1, 0:1]

        d_log = jnp.where(causal, bcol - brow + irow, -jnp.inf)
        inter = bcol + m_prev
        m_t = jnp.maximum(inter, jnp.max(d_log, axis=-1, keepdims=True))
        w_intra = jnp.exp(d_log - m_t)
        w_inter = jnp.exp(inter - m_t)
        a = _dot_nt(q, kf.astype(BF16)) * w_intra
        nrow = n_ref[h]
        num = w_inter * _dot(q, c_ref[h].astype(BF16)) + _dot(a.astype(BF16), v)
        den = w_inter * jnp.sum(qf * nrow, axis=-1, keepdims=True) + jnp.sum(a, axis=-1, keepdims=True)
        hh = num / jnp.maximum(jnp.abs(den), jnp.exp(-m_t))
        hn = _rms_rows(hh, ng_ref[h])
        y_ref[:, h * dh:(h + 1) * dh] = (hn * og_ref[:, h * dh:(h + 1) * dh]).astype(BF16)

        b_last = bcol[L - 1:L, :]
        g_col = b_last - bcol + icol
        m_new = jnp.maximum(b_last + m_prev, jnp.max(g_col, axis=0, keepdims=True))
        decay = jnp.exp(b_last + m_prev - m_new)
        kw = kf * jnp.exp(g_col - m_new)
        c_ref[h] = decay * c_ref[h] + _dot(kw.T.astype(BF16), v)
        n_ref[h] = decay * nrow + jnp.sum(kw, axis=0, keepdims=True)
        m_ref[h:h + 1, :] = jnp.broadcast_to(m_new, (1, LANES))

    ext_ref[0:halo, :] = ext_ref[L:L + halo, :]


def _mlstm(qk, v, og, ifc, ifr, conv_w, conv_b, gate_col, gate_row, norm_g, batch):
    t = qk.shape[0]
    L = SCAN_CHUNK
    d = MLSTM_HEADS * MLSTM_HD
    nc = t // batch // L
    row = lambda n: pl.BlockSpec((L, n), lambda b, c: (b * nc + c, 0))
    full = lambda a: pl.BlockSpec(a.shape, lambda b, c: (0,) * a.ndim)
    return pl.pallas_call(
        _mlstm_kernel,
        grid=(batch, nc),
        in_specs=[row(2 * d), row(d), row(d), row(LANES),
                  pl.BlockSpec((None, 8, L), lambda b, c: (b, 0, c)),
                  full(conv_w), full(conv_b), full(gate_col), full(gate_row), full(norm_g)],
        out_specs=row(d),
        out_shape=jax.ShapeDtypeStruct((t, d), BF16),
        scratch_shapes=[pltpu.VMEM((L + 8, 2 * d), F32),
                        pltpu.VMEM((MLSTM_HEADS, MLSTM_HD, MLSTM_HD), F32),
                        pltpu.VMEM((MLSTM_HEADS, 1, MLSTM_HD), F32),
                        pltpu.VMEM((8, LANES), F32)],
        compiler_params=pltpu.CompilerParams(dimension_semantics=("arbitrary", "arbitrary"),
                                             vmem_limit_bytes=VMEM_LIMIT),
        name="mlstm",
    )(qk, v, og, ifc, ifr, conv_w, conv_b, gate_col, gate_row, norm_g)


def _proj_b_kernel(x_ref, g_ref, w_ref, cosd_ref, sind_ref, gq_ref, gk_ref,
                   qt_ref, k_ref, vt_ref, z_ref, qit_ref, ki_ref, wit_ref):
    d = DSA_HEADS * DSA_HD
    h = _normed_input(x_ref, g_ref)
    cosd = cosd_ref[...]
    sind = sind_ref[...]
    cosi, sini = cosd, sind

    acc = _dot(h, w_ref[:, 0:d])
    for hd in range(DSA_HEADS):
        sl = slice(hd * DSA_HD, (hd + 1) * DSA_HD)
        xs = _rotate(_rms_rows(acc[:, sl], gq_ref[...]), cosd, sind) * (DSA_HD ** -0.5 * LOG2E)
        qt_ref[sl, :] = xs.T.astype(BF16)
    acc = _dot(h, w_ref[:, d:2 * d])
    for hd in range(DSA_HEADS):
        sl = slice(hd * DSA_HD, (hd + 1) * DSA_HD)
        k_ref[:, sl] = _rotate(_rms_rows(acc[:, sl], gk_ref[...]), cosd, sind).astype(BF16)
    acc = _dot(h, w_ref[:, 2 * d:3 * d])
    for hd in range(DSA_HEADS):
        sl = slice(hd * DSA_HD, (hd + 1) * DSA_HD)
        vt_ref[sl, :] = acc[:, sl].T.astype(BF16)
    z_ref[...] = _silu(_dot(h, w_ref[:, 3 * d:4 * d]))
    acc = _dot(h, w_ref[:, 4 * d:4 * d + IDX_HEADS * LANES])
    for hd in range(IDX_HEADS):
        sl = slice(hd * LANES, (hd + 1) * LANES)
        qit_ref[sl, :] = (_rotate(acc[:, sl], cosi, sini) * (IDX_HD ** -0.5)).T.astype(BF16)
    c0 = 4 * d + IDX_HEADS * LANES
    ki_ref[...] = _rotate(_dot(h, w_ref[:, c0:c0 + LANES]), cosi, sini).astype(BF16)
    wi = _dot(h, w_ref[:, c0 + LANES:c0 + 2 * LANES]) * (IDX_HEADS ** -0.5)
    wit_ref[...] = wi.T[0:IDX_HEADS, :]


def _proj_b(x2, g, w_b, cosd, sind, gq, gk):
    t = x2.shape[0]
    d = DSA_HEADS * DSA_HD
    tm = DSA_TILE
    nt = t // tm
    row = lambda n: pl.BlockSpec((tm, n), lambda i: (i, 0))
    full = lambda a: pl.BlockSpec(a.shape, lambda i: (0, 0))
    fmaj = lambda n: pl.BlockSpec((None, n, tm), lambda i: (i, 0, 0))
    sds = jax.ShapeDtypeStruct
    return pl.pallas_call(
        _proj_b_kernel,
        grid=(nt,),
        in_specs=[row(D_MODEL), full(g), full(w_b), row(LANES), row(LANES), full(gq), full(gk)],
        out_specs=[fmaj(d), row(d), fmaj(d), row(d), fmaj(IDX_HEADS * LANES), row(LANES), fmaj(IDX_HEADS)],
        out_shape=[sds((nt, d, tm), BF16), sds((t, d), BF16), sds((nt, d, tm), BF16), sds((t, d), F32),
                   sds((nt, IDX_HEADS * LANES, tm), BF16), sds((t, LANES), BF16),
                   sds((nt, IDX_HEADS, tm), F32)],
        compiler_params=pltpu.CompilerParams(dimension_semantics=("arbitrary",),
                                             vmem_limit_bytes=VMEM_LIMIT),
        name="proj_b",
    )(x2, g, w_b, cosd, sind, gq, gk)


def _sortable_key(x):
    bits = pltpu.bitcast(x, I32)
    return bits ^ ((bits >> 31) & jnp.int32(0x7FFFFFFF))


def _fold_rows(x, rows, op=jnp.add):
    acc = x[0:rows, :]
    for r in range(1, x.shape[0] // rows):
        acc = op(acc, x[r * rows:(r + 1) * rows, :])
    return acc


def _reduce_rows8(x8, op):
    for shift in (4, 2, 1):
        x8 = op(x8, pltpu.roll(x8, shift, 0))
    return x8[0:1, :]


def _dsa_kernel(qt_ref, qit_ref, wit_ref, zb_ref, ki_ref, k_hbm, vt_hbm, o_ref,
                kbuf, vbuf, key_ref, k16_ref, gmax_ref, acc_ref, m_ref, s_ref, mt_ref, sem,
                *, top_k):
    T = DSA_TILE
    C = SCORE_ROWS
    b = pl.program_id(0)
    j = pl.program_id(1)

    def kv_copies():
        return (pltpu.make_async_copy(k_hbm.at[b], kbuf, sem.at[0]),
                pltpu.make_async_copy(vt_hbm.at[b], vbuf, sem.at[1]))

    @pl.when(j == 0)
    def _():
        for cp in kv_copies():
            cp.start()

    nt = j + 1
    t_q = j * T + lax.broadcasted_iota(I32, (1, T), 1)
    vis_end = (t_q // VIS_CHUNK + 1) * VIS_CHUNK
    row_pos = lax.broadcasted_iota(I32, (T, T), 0)
    row_pos_c = lax.broadcasted_iota(I32, (C, T), 0)

    gmax_ref[...] = jnp.full((T, T), INT_MIN, I32)

    def score_tile(t):
        for c in range(T // C):
            rows = slice(c * C, (c + 1) * C)
            sc = jnp.zeros((C, T), F32)
            for h in range(IDX_HEADS):
                lg = _dot(ki_ref[t, rows, :], qit_ref[h * LANES:(h + 1) * LANES, :])
                sc = sc + jnp.maximum(lg, 0.0) * wit_ref[h:h + 1, :]
            key = jnp.where(t * T + c * C + row_pos_c < vis_end, _sortable_key(sc), INT_MIN)
            key_ref[t, rows, :] = key
            k16_ref[t, rows, :] = (key >> 16).astype(I16)
            gmax_ref[rows, :] = jnp.maximum(gmax_ref[rows, :], key)

    def score_body(pair, carry):
        score_tile(2 * pair)
        score_tile(2 * pair + 1)
        return carry

    lax.fori_loop(0, (nt + 1) // 2, score_body, 0)

    def count16_ge(cand_row):
        cand = jnp.broadcast_to(cand_row.astype(I16), (PACKED_ROWS, T))
        n_acc = 4

        def body(pair, accs):
            accs = list(accs)
            for u in range(2):
                for r in range(T // PACKED_ROWS):
                    blk = k16_ref[2 * pair + u, r * PACKED_ROWS:(r + 1) * PACKED_ROWS, :]
                    accs[r % n_acc] = accs[r % n_acc] + jnp.where(blk >= cand, jnp.int16(1), jnp.int16(0))
            return tuple(accs)

        zero = jnp.zeros((PACKED_ROWS, T), I16)
        accs = lax.fori_loop(0, (nt + 1) // 2, body, (zero,) * n_acc)
        acc = (accs[0] + accs[1]) + (accs[2] + accs[3])
        return jnp.sum(acc.astype(I32), axis=0, keepdims=True)

    gm = gmax_ref[...]
    lo0 = _reduce_rows8(_fold_rows(gm, SUBLANES, jnp.minimum), jnp.minimum) >> 16
    hi0 = _reduce_rows8(_fold_rows(gm, SUBLANES, jnp.maximum), jnp.maximum) >> 16
    span = jnp.max((hi0 - lo0).astype(F32)).astype(I32)
    n_iter = jnp.int32(0)
    for bit in range(16):
        n_iter = n_iter + jnp.where((span >> bit) > 0, 1, 0)

    def hi_body(i, carry):
        lo, hi = carry
        mid = lo + ((hi - lo + 1) >> 1)
        ok = count16_ge(mid) >= top_k
        return jnp.where(ok, mid, lo), jnp.where(ok, hi, mid - 1)

    thr_hi, _ = lax.fori_loop(0, n_iter, hi_body, (lo0, hi0))
    n_above = count16_ge(thr_hi + 1) * jnp.where(thr_hi == 32767, 0, 1)

    def low_body(t, carry):
        key = key_ref[t]
        low = (key & 0xFFFF) - 32768
        k16_ref[t] = jnp.where((key >> 16) == thr_hi, low, -32768).astype(I16)
        return carry

    lax.fori_loop(0, nt, low_body, 0)
    need_lo = top_k - n_above

    def lo_body(i, thr_u):
        cand_u = thr_u | lax.shift_left(jnp.int32(1), 15 - i)
        return jnp.where(count16_ge(cand_u - 32768) >= need_lo, cand_u, thr_u)

    thr_lo = lax.fori_loop(0, 16, lo_body, jnp.zeros((1, T), I32)) - 32768
    thr = thr_hi * 65536 + (thr_lo + 32768)

    def count32_body(t, carry):
        n_ge, n_gt = carry
        key = key_ref[t]
        n_ge = n_ge + _fold_rows(jnp.where(key >= thr, 1.0, 0.0), SUBLANES)
        n_gt = n_gt + _fold_rows(jnp.where(key > thr, 1.0, 0.0), SUBLANES)
        return n_ge, n_gt

    zero8 = jnp.zeros((SUBLANES, T), F32)
    n_ge, n_gt = lax.fori_loop(0, nt, count32_body, (zero8, zero8))
    n_ge = jnp.sum(n_ge, axis=0, keepdims=True)
    n_gt = jnp.sum(n_gt, axis=0, keepdims=True)
    need = float(top_k) - n_gt
    surplus = jnp.max(n_ge - n_gt - need)

    @pl.when(surplus > 0.0)
    def _():
        c_i = lax.broadcasted_iota(I32, (T, T), 1)
        prefix = jnp.where(c_i <= row_pos, 1.0, 0.0).astype(BF16)
        demoted = jnp.where(thr == INT_MIN, INT_MIN, thr - 1)

        def tie_body(t, seen):
            key = key_ref[t]
            eq = key == thr
            eqf = jnp.where(eq, 1.0, 0.0)
            incl = _dot(prefix, eqf.astype(BF16))
            rank = seen + incl - eqf
            key_ref[t] = jnp.where(eq & (rank >= need), demoted, key)
            return seen + incl[T - 1:T, :]

        lax.fori_loop(0, nt, tie_body, jnp.zeros((1, T), F32))

    def bias_body(t, carry):
        key = key_ref[t]
        keep = jnp.where(t * T + row_pos < vis_end, jnp.where(key >= thr, 0.0, NEG_BIG), NEG_BIG)
        key_ref[t] = pltpu.bitcast(keep.astype(F32), I32)
        return carry

    lax.fori_loop(0, nt, bias_body, 0)

    m_ref[...] = jnp.full(m_ref.shape, NEG_BIG, F32)
    acc_ref[...] = jnp.zeros_like(acc_ref)
    ones_rows = jnp.ones((PACKED_ROWS, T), BF16)

    @pl.when(j == 0)
    def _():
        for cp in kv_copies():
            cp.wait()

    def logits(t, h):
        sl = slice(h * DSA_HD, (h + 1) * DSA_HD)
        s = _dot(kbuf[t, :, sl], qt_ref[sl, :]) + pltpu.bitcast(key_ref[t], F32)
        s_ref[h] = s
        mt_ref[h] = _fold_rows(s, SUBLANES, jnp.maximum)

    def accumulate(t, h):
        sl = slice(h * DSA_HD, (h + 1) * DSA_HD)
        m_old = m_ref[h]
        m_new = jnp.maximum(m_old, jnp.max(mt_ref[h], axis=0, keepdims=True))
        alpha = jnp.exp2(m_old - m_new)
        p = jnp.exp2(s_ref[h] - m_new)
        v_aug = jnp.concatenate([vbuf[t, sl, :], ones_rows], axis=0)
        acc_ref[h] = alpha * acc_ref[h] + _dot(v_aug, p.astype(BF16))
        m_ref[h] = m_new

    for h in range(DSA_HEADS):
        logits(0, h)

    def attn_step(t):
        for h in range(DSA_HEADS):
            accumulate(t, h)
            logits(t + 1, h)

    def attn_body(pair, carry):
        attn_step(2 * pair)
        attn_step(2 * pair + 1)
        return carry

    lax.fori_loop(0, (nt - 1) // 2, attn_body, 0)

    @pl.when((nt - 1) % 2 == 1)
    def _():
        attn_step(nt - 2)

    for h in range(DSA_HEADS):
        accumulate(nt - 1, h)

    for h in range(DSA_HEADS):
        sl = slice(h * DSA_HD, (h + 1) * DSA_HD)
        out_t = acc_ref[h, 0:DSA_HD, :] / acc_ref[h, DSA_HD:DSA_HD + 1, :]
        o_ref[:, sl] = (out_t.T * zb_ref[:, sl]).astype(BF16)


def _dsa(qt, k, vt, zb, qit, ki, wit, batch):
    t, d = k.shape
    s = t // batch
    T = DSA_TILE
    nb = s // T
    top_k = min(DSA_TOPK, s // 4)
    assert top_k <= T and nb % 2 == 0
    k4 = k.reshape(batch, nb, T, d)
    vt4 = vt.reshape(batch, nb, d, T)
    ki4 = ki.reshape(batch, nb, T, LANES)
    fmaj = lambda n: pl.BlockSpec((None, n, T), lambda b, j: (b * nb + j, 0, 0))
    row = lambda n: pl.BlockSpec((T, n), lambda b, j: (b * nb + j, 0))
    return pl.pallas_call(
        functools.partial(_dsa_kernel, top_k=top_k),
        grid=(batch, nb),
        in_specs=[fmaj(d), fmaj(IDX_HEADS * LANES), fmaj(IDX_HEADS), row(d),
                  pl.BlockSpec((None, nb, T, LANES), lambda b, j: (b, 0, 0, 0), pipeline_mode=pl.Buffered(1)),
                  pl.BlockSpec(memory_space=pl.ANY), pl.BlockSpec(memory_space=pl.ANY)],
        out_specs=row(d),
        out_shape=jax.ShapeDtypeStruct((t, d), BF16),
        scratch_shapes=[pltpu.VMEM((nb, T, d), BF16), pltpu.VMEM((nb, d, T), BF16),
                        pltpu.VMEM((nb, T, T), I32), pltpu.VMEM((nb, T, T), I16),
                        pltpu.VMEM((T, T), I32),
                        pltpu.VMEM((DSA_HEADS, DSA_HD + PACKED_ROWS, T), F32),
                        pltpu.VMEM((DSA_HEADS, 1, T), F32),
                        pltpu.VMEM((DSA_HEADS, T, T), F32),
                        pltpu.VMEM((DSA_HEADS, SUBLANES, T), F32),
                        pltpu.SemaphoreType.DMA((2,))],
        compiler_params=pltpu.CompilerParams(dimension_semantics=("arbitrary", "arbitrary"),
                                             vmem_limit_bytes=VMEM_LIMIT),
        name="dsa",
    )(qt, qit, wit, zb, ki4, k4, vt4)


def _mem_kv_kernel(mem_ref, g_ref, w_ref, gk_ref, km_ref, vm_ref):
    d = MEM_HEADS * MEM_HD
    mh = _rms_rows(mem_ref[...], g_ref[...]).astype(BF16)
    kv = _dot(mh, w_ref[...])
    for h in range(MEM_HEADS):
        sl = slice(h * MEM_HD, (h + 1) * MEM_HD)
        km_ref[:, sl] = _rms_rows(kv[:, sl], gk_ref[...]).astype(BF16)
    vm_ref[...] = kv[:, d:2 * d].astype(BF16)


def _mem_kv(mem, g, w, gk):
    batch, nm, _ = mem.shape
    d = MEM_HEADS * MEM_HD
    full = lambda a: pl.BlockSpec(a.shape, lambda b: (0, 0))
    blk = lambda n: pl.BlockSpec((None, nm, n), lambda b: (b, 0, 0))
    return pl.pallas_call(
        _mem_kv_kernel,
        grid=(batch,),
        in_specs=[blk(D_MODEL), full(g), full(w), full(gk)],
        out_specs=[blk(d), blk(d)],
        out_shape=[jax.ShapeDtypeStruct((batch, nm, d), BF16)] * 2,
        compiler_params=pltpu.CompilerParams(dimension_semantics=("arbitrary",),
                                             vmem_limit_bytes=VMEM_LIMIT),
        name="mem_kv",
    )(mem, g, w, gk)


def _mem_attn_kernel(x_ref, g_ref, w_ref, gq_ref, km_ref, vm_ref, y_ref):
    d = MEM_HEADS * MEM_HD
    h = _normed_input(x_ref, g_ref)
    qm = _dot(h, w_ref[:, 0:d])
    zm = _dot(h, w_ref[:, d:2 * d])
    for hd in range(MEM_HEADS):
        sl = slice(hd * MEM_HD, (hd + 1) * MEM_HD)
        qs = (_rms_rows(qm[:, sl], gq_ref[...]) * (MEM_HD ** -0.5)).astype(BF16)
        s = _dot_nt(qs, km_ref[:, sl])
        p = jnp.exp(s - jnp.max(s, axis=-1, keepdims=True))
        o = _dot(p.astype(BF16), vm_ref[:, sl]) / jnp.sum(p, axis=-1, keepdims=True)
        y_ref[:, sl] = (o * _silu(zm[:, sl])).astype(BF16)


def _mem_attn(x2, g, w_m, gq, km, vm):
    t = x2.shape[0]
    batch, nm, d = km.shape
    tm = ROW_TILE
    per_b = t // batch // tm
    row = lambda n: pl.BlockSpec((tm, n), lambda i: (i, 0))
    full = lambda a: pl.BlockSpec(a.shape, lambda i: (0, 0))
    mem = pl.BlockSpec((None, nm, d), lambda i: (i // per_b, 0, 0))
    return pl.pallas_call(
        _mem_attn_kernel,
        grid=(t // tm,),
        in_specs=[row(D_MODEL), full(g), full(w_m), full(gq), mem, mem],
        out_specs=row(d),
        out_shape=jax.ShapeDtypeStruct((t, d), BF16),
        compiler_params=pltpu.CompilerParams(dimension_semantics=("arbitrary",),
                                             vmem_limit_bytes=VMEM_LIMIT),
        name="mem_attn",
    )(x2, g, w_m, gq, km, vm)


def _merge_kernel(x_ref, g_ref, wg_ref, ya_ref, yb_ref, ym_ref, wb_ref, wo_ref, o_ref):
    x = x_ref[...]
    h = _rms_rows(x, g_ref[...]).astype(BF16)
    merged = None
    for i, y_ref in enumerate((ya_ref, yb_ref, ym_ref)):
        gate = _sigmoid(_dot(h, wg_ref[:, i * D_MODEL:(i + 1) * D_MODEL]))
        term = gate * _dot(y_ref[...], wb_ref[i])
        merged = term if merged is None else merged + term
    o_ref[...] = x + _dot(merged.astype(BF16), wo_ref[...])


def _merge(x2, g, w_gate, ya, yb, ym, w_branch, w_out):
    t = x2.shape[0]
    tm = ROW_TILE
    row = lambda n: pl.BlockSpec((tm, n), lambda i: (i, 0))
    full = lambda a: pl.BlockSpec(a.shape, lambda i: (0,) * a.ndim)
    return pl.pallas_call(
        _merge_kernel,
        grid=(t // tm,),
        in_specs=[row(D_MODEL), full(g), full(w_gate), row(D_MODEL), row(D_MODEL), row(D_MODEL),
                  full(w_branch), full(w_out)],
        out_specs=row(D_MODEL),
        out_shape=jax.ShapeDtypeStruct((t, D_MODEL), F32),
        compiler_params=pltpu.CompilerParams(dimension_semantics=("arbitrary",),
                                             vmem_limit_bytes=VMEM_LIMIT),
        name="merge",
    )(x2, g, w_gate, ya, yb, ym, w_branch, w_out)


def _rope_tables(positions):
    half = DSA_HD // 2
    inv_freq = jnp.power(ROPE_THETA, -jnp.arange(half, dtype=F32) / half)
    ang = positions.astype(F32)[..., None] * inv_freq
    cos, sin = jnp.cos(ang), jnp.sin(ang)
    cos_t = jnp.concatenate([cos, cos], axis=-1)
    sin_t = jnp.concatenate([-sin, sin], axis=-1)
    return cos_t.reshape(-1, LANES), sin_t.reshape(-1, LANES)


def _pad_idx_cols(w, heads):
    dm = w.shape[0]
    half = IDX_HD // 2
    w5 = w.reshape(dm, heads, 2, half, 1)
    w5 = jnp.concatenate([w5, jnp.zeros_like(w5)], axis=-1)
    return w5.reshape(dm, heads * LANES)


def _pad_cols(w, n):
    return jnp.pad(w, ((0, 0), (0, n - w.shape[1])))


def _layer(x, mem, positions, norm_g, mem_norm_g, w_in, conv_w, conv_b, gate_b, mlstm_norm_g,
           dsa_q_norm_g, dsa_k_norm_g, mem_q_norm_g, mem_k_norm_g, w_mem_kv, w_branch, w_out):
    batch, seq, dm = x.shape
    x2 = x.reshape(batch * seq, dm)
    d = D_MODEL
    splits = (2 * d, d, 2 * MLSTM_HEADS, d, d, d, d, d, d, IDX_HEADS * IDX_HD, IDX_HD, IDX_HEADS,
              d, d, N_BRANCH * d)
    cuts = np.cumsum(np.array(splits))[:-1].tolist()
    (w_qk, w_va, w_if, w_oa, w_za, w_qb, w_kb, w_vb, w_zb, w_qi, w_ki, w_wi,
     w_qm, w_zm, w_gate) = jnp.split(w_in, cuts, axis=-1)
    g = norm_g.reshape(1, dm)

    w_a = jnp.concatenate([w_qk, w_va, w_oa, w_za], axis=1).astype(BF16)
    qk, va, og, ifc = _proj_a(x2, g, w_a, _pad_cols(w_if, LANES).astype(BF16))
    ifr = ifc[:, :8].reshape(batch, seq, 8).transpose(0, 2, 1)
    gb = gate_b.reshape(2 * MLSTM_HEADS)
    ya = _mlstm(qk, va, og, ifc, ifr, conv_w, conv_b.reshape(1, -1),
                _pad_cols(gb.reshape(1, -1), LANES), gb.reshape(-1, 1),
                mlstm_norm_g.reshape(MLSTM_HEADS, 1, MLSTM_HD), batch)

    w_b = jnp.concatenate([w_qb, w_kb, w_vb, w_zb, _pad_idx_cols(w_qi, IDX_HEADS),
                           _pad_idx_cols(w_ki, 1), _pad_cols(w_wi, LANES)], axis=1).astype(BF16)
    cosd, sind = _rope_tables(positions)
    qt, kb, vt, zb, qit, ki, wit = _proj_b(x2, g, w_b, cosd, sind,
                                           dsa_q_norm_g.reshape(1, -1), dsa_k_norm_g.reshape(1, -1))
    yb = _dsa(qt, kb, vt, zb, qit, ki, wit, batch)

    km, vm = _mem_kv(mem, mem_norm_g.reshape(1, dm), w_mem_kv.astype(BF16), mem_k_norm_g.reshape(1, -1))
    ym = _mem_attn(x2, g, jnp.concatenate([w_qm, w_zm], axis=1).astype(BF16),
                   mem_q_norm_g.reshape(1, -1), km, vm)

    out = _merge(x2, g, w_gate.astype(BF16), ya, yb, ym, w_branch.astype(BF16), w_out.astype(BF16))
    return out.reshape(batch, seq, dm)


def kernel(x, mem, positions, norm_g, mem_norm_g, w_in, conv_w, conv_b, mlstm_gate_b, mlstm_norm_g,
           dsa_q_norm_g, dsa_k_norm_g, mem_q_norm_g, mem_k_norm_g, w_mem_kv, w_branch, w_out):
    for l in range(norm_g.shape[0]):
        x = _layer(x, mem, positions, norm_g[l], mem_norm_g[l], w_in[l], conv_w[l], conv_b[l],
                   mlstm_gate_b[l], mlstm_norm_g[l], dsa_q_norm_g[l], dsa_k_norm_g[l],
                   mem_q_norm_g[l], mem_k_norm_g[l], w_mem_kv[l], w_branch[l], w_out[l])
    return x
```

```python
import functools
import math

import numpy as np
import jax
import jax.numpy as jnp
from jax import lax
from jax.experimental import pallas as pl
from jax.experimental.pallas import tpu as pltpu

F32 = jnp.float32
BF16 = jnp.bfloat16
I16 = jnp.int16
I32 = jnp.int32

EPS = 1e-6
ROPE_THETA = 10000.0
D_MODEL = 1024
MLSTM_HEADS = 4
MLSTM_HD = 256
CONV_W = 4
DSA_HEADS = 8
DSA_HD = 128
IDX_HEADS = 8
IDX_HD = 64
DSA_TOPK = 256
VIS_CHUNK = 64
MEM_HEADS = 4
MEM_HD = 256
N_BRANCH = 3

LANES = 128
SUBLANES = 8
PACKED_ROWS = 16
SCAN_CHUNK = 256
DSA_TILE = 256
SCORE_ROWS = 64
ROW_TILE = 256
NEG_BIG = -1e30
INT_MIN = -2 ** 31
LOG2E = math.log2(math.e)
VMEM_LIMIT = 56 * 1024 * 1024


def _sigmoid(x):
    return 1.0 / (1.0 + jnp.exp(-x))


def _silu(x):
    return x * _sigmoid(x)


def _log_sigmoid(x):
    return jnp.minimum(x, 0.0) - jnp.log(1.0 + jnp.exp(-jnp.abs(x)))


def _dot(a, b):
    return jnp.dot(a, b, preferred_element_type=F32)


def _dot_nt(a, b):
    return lax.dot_general(a, b, (((1,), (1,)), ((), ())), preferred_element_type=F32)


def _dot_tn(a, b):
    return lax.dot_general(a, b, (((0,), (0,)), ((), ())), preferred_element_type=F32)


def _rms_rows(x, g):
    ms = jnp.mean(x * x, axis=-1, keepdims=True)
    return x * lax.rsqrt(ms + EPS) * g


def _normed_input(x_ref, g_ref):
    return _rms_rows(x_ref[...], g_ref[...]).astype(BF16)


def _rotate(x, cos, sin_signed):
    return x * cos + pltpu.roll(x, LANES // 2, 1) * sin_signed


def _proj_a_kernel(x_ref, g_ref, w_ref, wif_ref, qk_ref, v_ref, og_ref, if_ref):
    d = MLSTM_HEADS * MLSTM_HD
    h = _normed_input(x_ref, g_ref)
    qk_ref[...] = _dot(h, w_ref[:, 0:2 * d])
    v_ref[...] = _dot(h, w_ref[:, 2 * d:3 * d]).astype(BF16)
    o = _dot(h, w_ref[:, 3 * d:4 * d])
    z = _dot(h, w_ref[:, 4 * d:5 * d])
    og_ref[...] = _sigmoid(o) * _silu(z)
    if_ref[...] = _dot(h, wif_ref[...])


def _proj_a(x2, g, w_a, w_if):
    t = x2.shape[0]
    d = MLSTM_HEADS * MLSTM_HD
    tm = ROW_TILE
    row = lambda n: pl.BlockSpec((tm, n), lambda i: (i, 0))
    full = lambda a: pl.BlockSpec(a.shape, lambda i: (0, 0))
    return pl.pallas_call(
        _proj_a_kernel,
        grid=(t // tm,),
        in_specs=[row(D_MODEL), full(g), full(w_a), full(w_if)],
        out_specs=[row(2 * d), row(d), row(d), row(LANES)],
        out_shape=[jax.ShapeDtypeStruct((t, 2 * d), F32), jax.ShapeDtypeStruct((t, d), BF16),
                   jax.ShapeDtypeStruct((t, d), F32), jax.ShapeDtypeStruct((t, LANES), F32)],
        compiler_params=pltpu.CompilerParams(dimension_semantics=("arbitrary",),
                                             vmem_limit_bytes=VMEM_LIMIT),
        name="proj_a",
    )(x2, g, w_a, w_if)


def _mlstm_kernel(qk_ref, v_ref, og_ref, ifc_ref, ifr_ref, cw_ref, cb_ref, gbc_ref, gbr_ref, ng_ref,
                  y_ref, ext_ref, c_ref, n_ref, m_ref):
    L = SCAN_CHUNK
    dh = MLSTM_HD
    d = MLSTM_HEADS * dh
    halo = 8

    @pl.when(pl.program_id(1) == 0)
    def _():
        ext_ref[0:halo, :] = jnp.zeros((halo, 2 * d), F32)
        c_ref[...] = jnp.zeros_like(c_ref)
        n_ref[...] = jnp.zeros_like(n_ref)
        m_ref[...] = jnp.zeros_like(m_ref)

    ext_ref[halo:halo + L, :] = qk_ref[...]

    def conv_silu(c0):
        u = cb_ref[:, c0:c0 + dh]
        for j in range(CONV_W):
            r0 = halo - (CONV_W - 1) + j
            u = u + cw_ref[j:j + 1, c0:c0 + dh] * ext_ref[r0:r0 + L, c0:c0 + dh]
        return _silu(u)

    row_i = lax.broadcasted_iota(I32, (L, L), 0)
    col_i = lax.broadcasted_iota(I32, (L, L), 1)
    causal = row_i >= col_i
    tril = jnp.where(causal, 1.0, 0.0).astype(F32)
    triu = jnp.where(col_i >= row_i, 1.0, 0.0).astype(F32)

    ifc = ifc_ref[...] + gbc_ref[...]
    ifr = ifr_ref[...] + gbr_ref[...]
    bcol_all = jnp.dot(tril, _log_sigmoid(ifc), preferred_element_type=F32,
                       precision=lax.Precision.HIGHEST)
    brow_all = jnp.dot(_log_sigmoid(ifr), triu, preferred_element_type=F32,
                       precision=lax.Precision.HIGHEST)

    for h in range(MLSTM_HEADS):
        qf = conv_silu(h * dh)
        kf = conv_silu(d + h * dh) * (dh ** -0.5)
        q = qf.astype(BF16)
        v = v_ref[:, h * dh:(h + 1) * dh]
        icol = ifc[:, h:h + 1]
        irow = ifr[h:h + 1, :]
        bcol = bcol_all[:, MLSTM_HEADS + h:MLSTM_HEADS + h + 1]
        brow = brow_all[MLSTM_HEADS + h:MLSTM_HEADS + h + 1, :]
        m_prev = m_ref[h:h + 1, 0:1]

        d_log = jnp.where(causal, bcol - brow + irow, -jnp.inf)
        inter = bcol + m_prev
        m_t = jnp.maximum(inter, jnp.max(d_log, axis=-1, keepdims=True))
        w_intra = jnp.exp(d_log - m_t)
        w_inter = jnp.exp(inter - m_t)
        a = _dot_nt(q, kf.astype(BF16)) * w_intra
        nrow = n_ref[h]
        num = w_inter * _dot(q, c_ref[h].astype(BF16)) + _dot(a.astype(BF16), v)
        den = w_inter * jnp.sum(qf * nrow, axis=-1, keepdims=True) + jnp.sum(a, axis=-1, keepdims=True)
        hh = num / jnp.maximum(jnp.abs(den), jnp.exp(-m_t))
        hn = _rms_rows(hh, ng_ref[h])
        y_ref[:, h * dh:(h + 1) * dh] = (hn * og_ref[:, h * dh:(h + 1) * dh]).astype(BF16)

        b_last = bcol[L - 1:L, :]
        g_col = b_last - bcol + icol
        m_new = jnp.maximum(b_last + m_prev, jnp.max(g_col, axis=0, keepdims=True))
        decay = jnp.exp(b_last + m_prev - m_new)
        kw = kf * jnp.exp(g_col - m_new)
        c_ref[h] = decay * c_ref[h] + _dot(kw.T.astype(BF16), v)
        n_ref[h] = decay * nrow + jnp.sum(kw, axis=0, keepdims=True)
        m_ref[h:h + 1, :] = jnp.broadcast_to(m_new, (1, LANES))

    ext_ref[0:halo, :] = ext_ref[L:L + halo, :]


def _mlstm(qk, v, og, ifc, ifr, conv_w, conv_b, gate_col, gate_row, norm_g, batch):
    t = qk.shape[0]
    L = SCAN_CHUNK
    d = MLSTM_HEADS * MLSTM_HD
    nc = t // batch // L
    row = lambda n: pl.BlockSpec((L, n), lambda b, c: (b * nc + c, 0))
    full = lambda a: pl.BlockSpec(a.shape, lambda b, c: (0,) * a.ndim)
    return pl.pallas_call(
        _mlstm_kernel,
        grid=(batch, nc),
        in_specs=[row(2 * d), row(d), row(d), row(LANES),
                  pl.BlockSpec((None, 8, L), lambda b, c: (b, 0, c)),
                  full(conv_w), full(conv_b), full(gate_col), full(gate_row), full(norm_g)],
        out_specs=row(d),
        out_shape=jax.ShapeDtypeStruct((t, d), BF16),
        scratch_shapes=[pltpu.VMEM((L + 8, 2 * d), F32),
                        pltpu.VMEM((MLSTM_HEADS, MLSTM_HD, MLSTM_HD), F32),
                        pltpu.VMEM((MLSTM_HEADS, 1, MLSTM_HD), F32),
                        pltpu.VMEM((8, LANES), F32)],
        compiler_params=pltpu.CompilerParams(dimension_semantics=("arbitrary", "arbitrary"),
                                             vmem_limit_bytes=VMEM_LIMIT),
        name="mlstm",
    )(qk, v, og, ifc, ifr, conv_w, conv_b, gate_col, gate_row, norm_g)


def _proj_b_kernel(x_ref, g_ref, w_ref, cosd_ref, sind_ref, gq_ref, gk_ref,
                   qt_ref, k_ref, vt_ref, z_ref, qit_ref, ki_ref, wit_ref):
    d = DSA_HEADS * DSA_HD
    h = _normed_input(x_ref, g_ref)
    cosd = cosd_ref[...]
    sind = sind_ref[...]
    cosi, sini = cosd, sind

    acc = _dot(h, w_ref[:, 0:d])
    for hd in range(DSA_HEADS):
        sl = slice(hd * DSA_HD, (hd + 1) * DSA_HD)
        xs = _rotate(_rms_rows(acc[:, sl], gq_ref[...]), cosd, sind) * (DSA_HD ** -0.5 * LOG2E)
        qt_ref[sl, :] = xs.T.astype(BF16)
    acc = _dot(h, w_ref[:, d:2 * d])
    for hd in range(DSA_HEADS):
        sl = slice(hd * DSA_HD, (hd + 1) * DSA_HD)
        k_ref[:, sl] = _rotate(_rms_rows(acc[:, sl], gk_ref[...]), cosd, sind).astype(BF16)
    acc = _dot(h, w_ref[:, 2 * d:3 * d])
    for hd in range(DSA_HEADS):
        sl = slice(hd * DSA_HD, (hd + 1) * DSA_HD)
        vt_ref[sl, :] = acc[:, sl].T.astype(BF16)
    z_ref[...] = _silu(_dot(h, w_ref[:, 3 * d:4 * d]))
    acc = _dot(h, w_ref[:, 4 * d:4 * d + IDX_HEADS * LANES])
    for hd in range(IDX_HEADS):
        sl = slice(hd * LANES, (hd + 1) * LANES)
        qit_ref[sl, :] = (_rotate(acc[:, sl], cosi, sini) * (IDX_HD ** -0.5)).T.astype(BF16)
    c0 = 4 * d + IDX_HEADS * LANES
    ki_ref[...] = _rotate(_dot(h, w_ref[:, c0:c0 + LANES]), cosi, sini).astype(BF16)
    wi = _dot(h, w_ref[:, c0 + LANES:c0 + 2 * LANES]) * (IDX_HEADS ** -0.5)
    wit_ref[...] = wi.T[0:IDX_HEADS, :]


def _proj_b(x2, g, w_b, cosd, sind, gq, gk):
    t = x2.shape[0]
    d = DSA_HEADS * DSA_HD
    tm = DSA_TILE
    nt = t // tm
    row = lambda n: pl.BlockSpec((tm, n), lambda i: (i, 0))
    full = lambda a: pl.BlockSpec(a.shape, lambda i: (0, 0))
    fmaj = lambda n: pl.BlockSpec((None, n, tm), lambda i: (i, 0, 0))
    sds = jax.ShapeDtypeStruct
    return pl.pallas_call(
        _proj_b_kernel,
        grid=(nt,),
        in_specs=[row(D_MODEL), full(g), full(w_b), row(LANES), row(LANES), full(gq), full(gk)],
        out_specs=[fmaj(d), row(d), fmaj(d), row(d), fmaj(IDX_HEADS * LANES), row(LANES), fmaj(IDX_HEADS)],
        out_shape=[sds((nt, d, tm), BF16), sds((t, d), BF16), sds((nt, d, tm), BF16), sds((t, d), F32),
                   sds((nt, IDX_HEADS * LANES, tm), BF16), sds((t, LANES), BF16),
                   sds((nt, IDX_HEADS, tm), F32)],
        compiler_params=pltpu.CompilerParams(dimension_semantics=("arbitrary",),
                                             vmem_limit_bytes=VMEM_LIMIT),
        name="proj_b",
    )(x2, g, w_b, cosd, sind, gq, gk)


def _sortable_key(x):
    bits = pltpu.bitcast(x, I32)
    return bits ^ ((bits >> 31) & jnp.int32(0x7FFFFFFF))


def _fold_rows(x, rows, op=jnp.add):
    acc = x[0:rows, :]
    for r in range(1, x.shape[0] // rows):
        acc = op(acc, x[r * rows:(r + 1) * rows, :])
    return acc


def _reduce_rows8(x8, op):
    for shift in (4, 2, 1):
        x8 = op(x8, pltpu.roll(x8, shift, 0))
    return x8[0:1, :]


def _dsa_kernel(qt_ref, qit_ref, wit_ref, zb_ref, ki_ref, k_hbm, vt_hbm, o_ref,
                kbuf, vbuf, key_ref, k16_ref, gmax_ref, acc_ref, m_ref, s_ref, mt_ref, sem,
                *, top_k):
    T = DSA_TILE
    C = SCORE_ROWS
    b = pl.program_id(0)
    j = pl.program_id(1)

    def kv_copies():
        return (pltpu.make_async_copy(k_hbm.at[b], kbuf, sem.at[0]),
                pltpu.make_async_copy(vt_hbm.at[b], vbuf, sem.at[1]))

    @pl.when(j == 0)
    def _():
        for cp in kv_copies():
            cp.start()

    nt = j + 1
    t_q = j * T + lax.broadcasted_iota(I32, (1, T), 1)
    vis_end = (t_q // VIS_CHUNK + 1) * VIS_CHUNK
    row_pos = lax.broadcasted_iota(I32, (T, T), 0)
    row_pos_c = lax.broadcasted_iota(I32, (C, T), 0)

    gmax_ref[...] = jnp.full((T, T), INT_MIN, I32)

    def score_tile(t, diagonal):
        for c in range(T // C):
            rows = slice(c * C, (c + 1) * C)
            sc = jnp.zeros((C, T), F32)
            for h in range(IDX_HEADS):
                lg = _dot(ki_ref[t, rows, :], qit_ref[h * LANES:(h + 1) * LANES, :])
                sc = sc + jnp.maximum(lg, 0.0) * wit_ref[h:h + 1, :]
            key = _sortable_key(sc)
            if diagonal:
                key = jnp.where(t * T + c * C + row_pos_c < vis_end, key, INT_MIN)
            key_ref[t, rows, :] = key
            k16_ref[t, rows, :] = (key >> 16).astype(I16)
            gmax_ref[rows, :] = jnp.maximum(gmax_ref[rows, :], key)

    def score_body(pair, carry):
        score_tile(2 * pair, False)
        score_tile(2 * pair + 1, False)
        return carry

    lax.fori_loop(0, (nt - 1) // 2, score_body, 0)

    @pl.when(nt % 2 == 0)
    def _():
        score_tile(nt - 2, False)

    score_tile(nt - 1, True)

    @pl.when(nt % 2 == 1)
    def _():
        k16_ref[nt] = jnp.full((T, T), -32768, I16)

    def count16_ge(cand_row):
        cand = jnp.broadcast_to(cand_row.astype(I16), (PACKED_ROWS, T))
        n_acc = 4

        def body(pair, accs):
            accs = list(accs)
            for u in range(2):
                for r in range(T // PACKED_ROWS):
                    blk = k16_ref[2 * pair + u, r * PACKED_ROWS:(r + 1) * PACKED_ROWS, :]
                    accs[r % n_acc] = accs[r % n_acc] + jnp.where(blk >= cand, jnp.int16(1), jnp.int16(0))
            return tuple(accs)

        zero = jnp.zeros((PACKED_ROWS, T), I16)
        accs = lax.fori_loop(0, (nt + 1) // 2, body, (zero,) * n_acc)
        acc = (accs[0] + accs[1]) + (accs[2] + accs[3])
        return jnp.sum(acc.astype(I32), axis=0, keepdims=True)

    gm = gmax_ref[...]
    lo0 = _reduce_rows8(_fold_rows(gm, SUBLANES, jnp.minimum), jnp.minimum) >> 16
    hi0 = _reduce_rows8(_fold_rows(gm, SUBLANES, jnp.maximum), jnp.maximum) >> 16
    span = jnp.max((hi0 - lo0).astype(F32)).astype(I32)
    n_iter = jnp.int32(0)
    for bit in range(16):
        n_iter = n_iter + jnp.where((span >> bit) > 0, 1, 0)

    def hi_body(i, carry):
        lo, hi = carry
        mid = lo + ((hi - lo + 1) >> 1)
        ok = count16_ge(mid) >= top_k
        return jnp.where(ok, mid, lo), jnp.where(ok, hi, mid - 1)

    thr_hi, _ = lax.fori_loop(0, n_iter, hi_body, (lo0, hi0))
    n_above = count16_ge(thr_hi + 1) * jnp.where(thr_hi == 32767, 0, 1)

    def low_body(t, carry):
        key = key_ref[t]
        low = (key & 0xFFFF) - 32768
        k16_ref[t] = jnp.where((key >> 16) == thr_hi, low, -32768).astype(I16)
        return carry

    lax.fori_loop(0, nt, low_body, 0)
    need_lo = top_k - n_above

    def lo_body(i, thr_u):
        cand_u = thr_u | lax.shift_left(jnp.int32(1), 15 - i)
        return jnp.where(count16_ge(cand_u - 32768) >= need_lo, cand_u, thr_u)

    thr_lo = lax.fori_loop(0, 16, lo_body, jnp.zeros((1, T), I32)) - 32768
    thr = thr_hi * 65536 + (thr_lo + 32768)

    n_gt = n_above + count16_ge(thr_lo + 1) * jnp.where(thr_lo == 32767, 0, 1)
    n_ge = n_above + count16_ge(thr_lo)
    need = (top_k - n_gt).astype(F32)
    surplus = jnp.max(jnp.where(thr_lo == -32768, 1, n_ge - top_k).astype(F32))

    @pl.when(surplus > 0.0)
    def _():
        c_i = lax.broadcasted_iota(I32, (T, T), 1)
        prefix = jnp.where(c_i <= row_pos, 1.0, 0.0).astype(BF16)
        demoted = jnp.where(thr == INT_MIN, INT_MIN, thr - 1)

        def tie_body(t, seen):
            key = key_ref[t]
            eq = key == thr
            eqf = jnp.where(eq, 1.0, 0.0)
            incl = _dot(prefix, eqf.astype(BF16))
            rank = seen + incl - eqf
            key_ref[t] = jnp.where(eq & (rank >= need), demoted, key)
            return seen + incl[T - 1:T, :]

        lax.fori_loop(0, nt, tie_body, jnp.zeros((1, T), F32))

    def bias_tile(t, diagonal):
        keep = jnp.where(key_ref[t] >= thr, 0.0, NEG_BIG)
        if diagonal:
            keep = jnp.where(t * T + row_pos < vis_end, keep, NEG_BIG)
        key_ref[t] = pltpu.bitcast(keep.astype(F32), I32)

    def bias_body(t, carry):
        bias_tile(t, False)
        return carry

    lax.fori_loop(0, nt - 1, bias_body, 0)
    bias_tile(nt - 1, True)

    m_ref[...] = jnp.full(m_ref.shape, NEG_BIG, F32)
    acc_ref[...] = jnp.zeros_like(acc_ref)
    ones_rows = jnp.ones((PACKED_ROWS, T), BF16)

    @pl.when(j == 0)
    def _():
        for cp in kv_copies():
            cp.wait()

    def logits(t, h):
        sl = slice(h * DSA_HD, (h + 1) * DSA_HD)
        s = _dot(kbuf[t, :, sl], qt_ref[sl, :]) + pltpu.bitcast(key_ref[t], F32)
        s_ref[h] = s
        mt_ref[h] = _fold_rows(s, SUBLANES, jnp.maximum)

    def accumulate(t, h):
        sl = slice(h * DSA_HD, (h + 1) * DSA_HD)
        m_old = m_ref[h]
        m_new = jnp.maximum(m_old, jnp.max(mt_ref[h], axis=0, keepdims=True))
        alpha = jnp.exp2(m_old - m_new)
        p = jnp.exp2(s_ref[h] - m_new)
        v_aug = jnp.concatenate([vbuf[t, sl, :], ones_rows], axis=0)
        acc_ref[h] = alpha * acc_ref[h] + _dot(v_aug, p.astype(BF16))
        m_ref[h] = m_new

    for h in range(DSA_HEADS):
        logits(0, h)

    def attn_step(t):
        for h in range(DSA_HEADS):
            accumulate(t, h)
            logits(t + 1, h)

    def attn_body(pair, carry):
        attn_step(2 * pair)
        attn_step(2 * pair + 1)
        return carry

    lax.fori_loop(0, (nt - 1) // 2, attn_body, 0)

    @pl.when((nt - 1) % 2 == 1)
    def _():
        attn_step(nt - 2)

    for h in range(DSA_HEADS):
        accumulate(nt - 1, h)

    for h in range(DSA_HEADS):
        sl = slice(h * DSA_HD, (h + 1) * DSA_HD)
        out_t = acc_ref[h, 0:DSA_HD, :] / acc_ref[h, DSA_HD:DSA_HD + 1, :]
        o_ref[:, sl] = (out_t.T * zb_ref[:, sl]).astype(BF16)


def _dsa(qt, k, vt, zb, qit, ki, wit, batch):
    t, d = k.shape
    s = t // batch
    T = DSA_TILE
    nb = s // T
    top_k = min(DSA_TOPK, s // 4)
    assert top_k <= T and nb % 2 == 0
    k4 = k.reshape(batch, nb, T, d)
    vt4 = vt.reshape(batch, nb, d, T)
    ki4 = ki.reshape(batch, nb, T, LANES)
    fmaj = lambda n: pl.BlockSpec((None, n, T), lambda b, j: (b * nb + j, 0, 0))
    row = lambda n: pl.BlockSpec((T, n), lambda b, j: (b * nb + j, 0))
    return pl.pallas_call(
        functools.partial(_dsa_kernel, top_k=top_k),
        grid=(batch, nb),
        in_specs=[fmaj(d), fmaj(IDX_HEADS * LANES), fmaj(IDX_HEADS), row(d),
                  pl.BlockSpec((None, nb, T, LANES), lambda b, j: (b, 0, 0, 0), pipeline_mode=pl.Buffered(1)),
                  pl.BlockSpec(memory_space=pl.ANY), pl.BlockSpec(memory_space=pl.ANY)],
        out_specs=row(d),
        out_shape=jax.ShapeDtypeStruct((t, d), BF16),
        scratch_shapes=[pltpu.VMEM((nb, T, d), BF16), pltpu.VMEM((nb, d, T), BF16),
                        pltpu.VMEM((nb, T, T), I32), pltpu.VMEM((nb, T, T), I16),
                        pltpu.VMEM((T, T), I32),
                        pltpu.VMEM((DSA_HEADS, DSA_HD + PACKED_ROWS, T), F32),
                        pltpu.VMEM((DSA_HEADS, 1, T), F32),
                        pltpu.VMEM((DSA_HEADS, T, T), F32),
                        pltpu.VMEM((DSA_HEADS, SUBLANES, T), F32),
                        pltpu.SemaphoreType.DMA((2,))],
        compiler_params=pltpu.CompilerParams(dimension_semantics=("arbitrary", "arbitrary"),
                                             vmem_limit_bytes=VMEM_LIMIT),
        name="dsa",
    )(qt, qit, wit, zb, ki4, k4, vt4)


def _mem_kv_kernel(mem_ref, g_ref, w_ref, gk_ref, km_ref, vm_ref):
    d = MEM_HEADS * MEM_HD
    mh = _rms_rows(mem_ref[...], g_ref[...]).astype(BF16)
    kv = _dot(mh, w_ref[...])
    for h in range(MEM_HEADS):
        sl = slice(h * MEM_HD, (h + 1) * MEM_HD)
        km_ref[:, sl] = _rms_rows(kv[:, sl], gk_ref[...]).astype(BF16)
    vm_ref[...] = kv[:, d:2 * d].astype(BF16)


def _mem_kv(mem, g, w, gk):
    batch, nm, _ = mem.shape
    d = MEM_HEADS * MEM_HD
    full = lambda a: pl.BlockSpec(a.shape, lambda b: (0, 0))
    blk = lambda n: pl.BlockSpec((None, nm, n), lambda b: (b, 0, 0))
    return pl.pallas_call(
        _mem_kv_kernel,
        grid=(batch,),
        in_specs=[blk(D_MODEL), full(g), full(w), full(gk)],
        out_specs=[blk(d), blk(d)],
        out_shape=[jax.ShapeDtypeStruct((batch, nm, d), BF16)] * 2,
        compiler_params=pltpu.CompilerParams(dimension_semantics=("arbitrary",),
                                             vmem_limit_bytes=VMEM_LIMIT),
        name="mem_kv",
    )(mem, g, w, gk)


def _mem_attn_kernel(x_ref, g_ref, w_ref, gq_ref, km_ref, vm_ref, y_ref):
    d = MEM_HEADS * MEM_HD
    h = _normed_input(x_ref, g_ref)
    qm = _dot(h, w_ref[:, 0:d])
    zm = _dot(h, w_ref[:, d:2 * d])
    for hd in range(MEM_HEADS):
        sl = slice(hd * MEM_HD, (hd + 1) * MEM_HD)
        qs = (_rms_rows(qm[:, sl], gq_ref[...]) * (MEM_HD ** -0.5)).astype(BF16)
        s = _dot_nt(qs, km_ref[:, sl])
        p = jnp.exp(s - jnp.max(s, axis=-1, keepdims=True))
        o = _dot(p.astype(BF16), vm_ref[:, sl]) / jnp.sum(p, axis=-1, keepdims=True)
        y_ref[:, sl] = (o * _silu(zm[:, sl])).astype(BF16)


def _mem_attn(x2, g, w_m, gq, km, vm):
    t = x2.shape[0]
    batch, nm, d = km.shape
    tm = ROW_TILE
    per_b = t // batch // tm
    row = lambda n: pl.BlockSpec((tm, n), lambda i: (i, 0))
    full = lambda a: pl.BlockSpec(a.shape, lambda i: (0, 0))
    mem = pl.BlockSpec((None, nm, d), lambda i: (i // per_b, 0, 0))
    return pl.pallas_call(
        _mem_attn_kernel,
        grid=(t // tm,),
        in_specs=[row(D_MODEL), full(g), full(w_m), full(gq), mem, mem],
        out_specs=row(d),
        out_shape=jax.ShapeDtypeStruct((t, d), BF16),
        compiler_params=pltpu.CompilerParams(dimension_semantics=("arbitrary",),
                                             vmem_limit_bytes=VMEM_LIMIT),
        name="mem_attn",
    )(x2, g, w_m, gq, km, vm)


def _merge_kernel(x_ref, g_ref, wg_ref, ya_ref, yb_ref, ym_ref, wb_ref, wo_ref, o_ref):
    x = x_ref[...]
    h = _rms_rows(x, g_ref[...]).astype(BF16)
    merged = None
    for i, y_ref in enumerate((ya_ref, yb_ref, ym_ref)):
        gate = _sigmoid(_dot(h, wg_ref[:, i * D_MODEL:(i + 1) * D_MODEL]))
        term = gate * _dot(y_ref[...], wb_ref[i])
        merged = term if merged is None else merged + term
    o_ref[...] = x + _dot(merged.astype(BF16), wo_ref[...])


def _merge(x2, g, w_gate, ya, yb, ym, w_branch, w_out):
    t = x2.shape[0]
    tm = ROW_TILE
    row = lambda n: pl.BlockSpec((tm, n), lambda i: (i, 0))
    full = lambda a: pl.BlockSpec(a.shape, lambda i: (0,) * a.ndim)
    return pl.pallas_call(
        _merge_kernel,
        grid=(t // tm,),
        in_specs=[row(D_MODEL), full(g), full(w_gate), row(D_MODEL), row(D_MODEL), row(D_MODEL),
                  full(w_branch), full(w_out)],
        out_specs=row(D_MODEL),
        out_shape=jax.ShapeDtypeStruct((t, D_MODEL), F32),
        compiler_params=pltpu.CompilerParams(dimension_semantics=("arbitrary",),
                                             vmem_limit_bytes=VMEM_LIMIT),
        name="merge",
    )(x2, g, w_gate, ya, yb, ym, w_branch, w_out)


def _rope_tables(positions):
    half = DSA_HD // 2
    inv_freq = jnp.power(ROPE_THETA, -jnp.arange(half, dtype=F32) / half)
    ang = positions.astype(F32)[..., None] * inv_freq
    cos, sin = jnp.cos(ang), jnp.sin(ang)
    cos_t = jnp.concatenate([cos, cos], axis=-1)
    sin_t = jnp.concatenate([-sin, sin], axis=-1)
    return cos_t.reshape(-1, LANES), sin_t.reshape(-1, LANES)


def _pad_idx_cols(w, heads):
    dm = w.shape[0]
    half = IDX_HD // 2
    w5 = w.reshape(dm, heads, 2, half, 1)
    w5 = jnp.concatenate([w5, jnp.zeros_like(w5)], axis=-1)
    return w5.reshape(dm, heads * LANES)


def _pad_cols(w, n):
    return jnp.pad(w, ((0, 0), (0, n - w.shape[1])))


def _layer(x, mem, positions, norm_g, mem_norm_g, w_in, conv_w, conv_b, gate_b, mlstm_norm_g,
           dsa_q_norm_g, dsa_k_norm_g, mem_q_norm_g, mem_k_norm_g, w_mem_kv, w_branch, w_out):
    batch, seq, dm = x.shape
    x2 = x.reshape(batch * seq, dm)
    d = D_MODEL
    splits = (2 * d, d, 2 * MLSTM_HEADS, d, d, d, d, d, d, IDX_HEADS * IDX_HD, IDX_HD, IDX_HEADS,
              d, d, N_BRANCH * d)
    cuts = np.cumsum(np.array(splits))[:-1].tolist()
    (w_qk, w_va, w_if, w_oa, w_za, w_qb, w_kb, w_vb, w_zb, w_qi, w_ki, w_wi,
     w_qm, w_zm, w_gate) = jnp.split(w_in, cuts, axis=-1)
    g = norm_g.reshape(1, dm)

    w_a = jnp.concatenate([w_qk, w_va, w_oa, w_za], axis=1).astype(BF16)
    qk, va, og, ifc = _proj_a(x2, g, w_a, _pad_cols(w_if, LANES).astype(BF16))
    ifr = ifc[:, :8].reshape(batch, seq, 8).transpose(0, 2, 1)
    gb = gate_b.reshape(2 * MLSTM_HEADS)
    ya = _mlstm(qk, va, og, ifc, ifr, conv_w, conv_b.reshape(1, -1),
                _pad_cols(gb.reshape(1, -1), LANES), gb.reshape(-1, 1),
                mlstm_norm_g.reshape(MLSTM_HEADS, 1, MLSTM_HD), batch)

    w_b = jnp.concatenate([w_qb, w_kb, w_vb, w_zb, _pad_idx_cols(w_qi, IDX_HEADS),
                           _pad_idx_cols(w_ki, 1), _pad_cols(w_wi, LANES)], axis=1).astype(BF16)
    cosd, sind = _rope_tables(positions)
    qt, kb, vt, zb, qit, ki, wit = _proj_b(x2, g, w_b, cosd, sind,
                                           dsa_q_norm_g.reshape(1, -1), dsa_k_norm_g.reshape(1, -1))
    yb = _dsa(qt, kb, vt, zb, qit, ki, wit, batch)

    km, vm = _mem_kv(mem, mem_norm_g.reshape(1, dm), w_mem_kv.astype(BF16), mem_k_norm_g.reshape(1, -1))
    ym = _mem_attn(x2, g, jnp.concatenate([w_qm, w_zm], axis=1).astype(BF16),
                   mem_q_norm_g.reshape(1, -1), km, vm)

    out = _merge(x2, g, w_gate.astype(BF16), ya, yb, ym, w_branch.astype(BF16), w_out.astype(BF16))
    return out.reshape(batch, seq, dm)


def kernel(x, mem, positions, norm_g, mem_norm_g, w_in, conv_w, conv_b, mlstm_gate_b, mlstm_norm_g,
           dsa_q_norm_g, dsa_k_norm_g, mem_q_norm_g, mem_k_norm_g, w_mem_kv, w_branch, w_out):
    for l in range(norm_g.shape[0]):
        x = _layer(x, mem, positions, norm_g[l], mem_norm_g[l], w_in[l], conv_w[l], conv_b[l],
                   mlstm_gate_b[l], mlstm_norm_g[l], dsa_q_norm_g[l], dsa_k_norm_g[l],
                   mem_q_norm_g[l], mem_k_norm_g[l], w_mem_kv[l], w_branch[l], w_out[l])
    return x
```

```python
import functools
import math

import numpy as np
import jax
import jax.numpy as jnp
from jax import lax
from jax.experimental import pallas as pl
from jax.experimental.pallas import tpu as pltpu

F32 = jnp.float32
BF16 = jnp.bfloat16
I16 = jnp.int16
I32 = jnp.int32

EPS = 1e-6
ROPE_THETA = 10000.0
D_MODEL = 1024
MLSTM_HEADS = 4
MLSTM_HD = 256
CONV_W = 4
DSA_HEADS = 8
DSA_HD = 128
IDX_HEADS = 8
IDX_HD = 64
DSA_TOPK = 256
VIS_CHUNK = 64
MEM_HEADS = 4
MEM_HD = 256
N_BRANCH = 3

LANES = 128
SUBLANES = 8
PACKED_ROWS = 16
SCAN_CHUNK = 256
DSA_TILE = 256
SCORE_ROWS = 64
ROW_TILE = 256
NEG_BIG = -1e30
INT_MIN = -2 ** 31
LOG2E = math.log2(math.e)
VMEM_LIMIT = 56 * 1024 * 1024


def _sigmoid(x):
    return 1.0 / (1.0 + jnp.exp(-x))


def _silu(x):
    return x * _sigmoid(x)


def _log_sigmoid(x):
    return jnp.minimum(x, 0.0) - jnp.log(1.0 + jnp.exp(-jnp.abs(x)))


def _dot(a, b):
    return jnp.dot(a, b, preferred_element_type=F32)


def _dot_nt(a, b):
    return lax.dot_general(a, b, (((1,), (1,)), ((), ())), preferred_element_type=F32)


def _dot_tn(a, b):
    return lax.dot_general(a, b, (((0,), (0,)), ((), ())), preferred_element_type=F32)


def _rms_rows(x, g):
    ms = jnp.mean(x * x, axis=-1, keepdims=True)
    return x * lax.rsqrt(ms + EPS) * g


def _normed_input(x_ref, g_ref):
    return _rms_rows(x_ref[...], g_ref[...]).astype(BF16)


def _rotate(x, cos, sin_signed):
    return x * cos + pltpu.roll(x, LANES // 2, 1) * sin_signed


def _proj_a_kernel(x_ref, g_ref, w_ref, wif_ref, qk_ref, v_ref, og_ref, if_ref, h_ref):
    d = MLSTM_HEADS * MLSTM_HD
    h = _normed_input(x_ref, g_ref)
    h_ref[...] = h
    qk_ref[...] = _dot(h, w_ref[:, 0:2 * d])
    v_ref[...] = _dot(h, w_ref[:, 2 * d:3 * d]).astype(BF16)
    o = _dot(h, w_ref[:, 3 * d:4 * d])
    z = _dot(h, w_ref[:, 4 * d:5 * d])
    og_ref[...] = _sigmoid(o) * _silu(z)
    if_ref[...] = _dot(h, wif_ref[...])


def _proj_a(x2, g, w_a, w_if):
    t = x2.shape[0]
    d = MLSTM_HEADS * MLSTM_HD
    tm = ROW_TILE
    row = lambda n: pl.BlockSpec((tm, n), lambda i: (i, 0))
    full = lambda a: pl.BlockSpec(a.shape, lambda i: (0, 0))
    return pl.pallas_call(
        _proj_a_kernel,
        grid=(t // tm,),
        in_specs=[row(D_MODEL), full(g), full(w_a), full(w_if)],
        out_specs=[row(2 * d), row(d), row(d), row(LANES), row(D_MODEL)],
        out_shape=[jax.ShapeDtypeStruct((t, 2 * d), F32), jax.ShapeDtypeStruct((t, d), BF16),
                   jax.ShapeDtypeStruct((t, d), F32), jax.ShapeDtypeStruct((t, LANES), F32),
                   jax.ShapeDtypeStruct((t, D_MODEL), BF16)],
        compiler_params=pltpu.CompilerParams(dimension_semantics=("arbitrary",),
                                             vmem_limit_bytes=VMEM_LIMIT),
        name="proj_a",
    )(x2, g, w_a, w_if)


def _mlstm_kernel(qk_ref, v_ref, og_ref, ifc_ref, ifr_ref, cw_ref, cb_ref, gbc_ref, gbr_ref, ng_ref,
                  y_ref, ext_ref, c_ref, n_ref, m_ref):
    L = SCAN_CHUNK
    dh = MLSTM_HD
    d = MLSTM_HEADS * dh
    halo = 8

    @pl.when(pl.program_id(1) == 0)
    def _():
        ext_ref[0:halo, :] = jnp.zeros((halo, 2 * d), F32)
        c_ref[...] = jnp.zeros_like(c_ref)
        n_ref[...] = jnp.zeros_like(n_ref)
        m_ref[...] = jnp.zeros_like(m_ref)

    ext_ref[halo:halo + L, :] = qk_ref[...]

    def conv_silu(c0):
        u = cb_ref[:, c0:c0 + dh]
        for j in range(CONV_W):
            r0 = halo - (CONV_W - 1) + j
            u = u + cw_ref[j:j + 1, c0:c0 + dh] * ext_ref[r0:r0 + L, c0:c0 + dh]
        return _silu(u)

    row_i = lax.broadcasted_iota(I32, (L, L), 0)
    col_i = lax.broadcasted_iota(I32, (L, L), 1)
    causal = row_i >= col_i
    tril = jnp.where(causal, 1.0, 0.0).astype(F32)
    triu = jnp.where(col_i >= row_i, 1.0, 0.0).astype(F32)

    ifc = ifc_ref[...] + gbc_ref[...]
    ifr = ifr_ref[...] + gbr_ref[...]
    bcol_all = jnp.dot(tril, _log_sigmoid(ifc), preferred_element_type=F32,
                       precision=lax.Precision.HIGHEST)
    brow_all = jnp.dot(_log_sigmoid(ifr), triu, preferred_element_type=F32,
                       precision=lax.Precision.HIGHEST)

    for h in range(MLSTM_HEADS):
        qf = conv_silu(h * dh)
        kf = conv_silu(d + h * dh) * (dh ** -0.5)
        q = qf.astype(BF16)
        v = v_ref[:, h * dh:(h + 1) * dh]
        icol = ifc[:, h:h + 1]
        irow = ifr[h:h + 1, :]
        bcol = bcol_all[:, MLSTM_HEADS + h:MLSTM_HEADS + h + 1]
        brow = brow_all[MLSTM_HEADS + h:MLSTM_HEADS + h + 1, :]
        m_prev = m_ref[h:h + 1, 0:1]

        d_log = jnp.where(causal, bcol - brow + irow, -jnp.inf)
        inter = bcol + m_prev
        m_t = jnp.maximum(inter, jnp.max(d_log, axis=-1, keepdims=True))
        w_intra = jnp.exp(d_log - m_t)
        w_inter = jnp.exp(inter - m_t)
        a = _dot_nt(q, kf.astype(BF16)) * w_intra
        nrow = n_ref[h]
        num = w_inter * _dot(q, c_ref[h].astype(BF16)) + _dot(a.astype(BF16), v)
        den = w_inter * jnp.sum(qf * nrow, axis=-1, keepdims=True) + jnp.sum(a, axis=-1, keepdims=True)
        hh = num / jnp.maximum(jnp.abs(den), jnp.exp(-m_t))
        hn = _rms_rows(hh, ng_ref[h])
        y_ref[:, h * dh:(h + 1) * dh] = (hn * og_ref[:, h * dh:(h + 1) * dh]).astype(BF16)

        b_last = bcol[L - 1:L, :]
        g_col = b_last - bcol + icol
        m_new = jnp.maximum(b_last + m_prev, jnp.max(g_col, axis=0, keepdims=True))
        decay = jnp.exp(b_last + m_prev - m_new)
        kw = kf * jnp.exp(g_col - m_new)
        c_ref[h] = decay * c_ref[h] + _dot(kw.T.astype(BF16), v)
        n_ref[h] = decay * nrow + jnp.sum(kw, axis=0, keepdims=True)
        m_ref[h:h + 1, :] = jnp.broadcast_to(m_new, (1, LANES))

    ext_ref[0:halo, :] = ext_ref[L:L + halo, :]


def _mlstm(qk, v, og, ifc, ifr, conv_w, conv_b, gate_col, gate_row, norm_g, batch):
    t = qk.shape[0]
    L = SCAN_CHUNK
    d = MLSTM_HEADS * MLSTM_HD
    nc = t // batch // L
    row = lambda n: pl.BlockSpec((L, n), lambda b, c: (b * nc + c, 0))
    full = lambda a: pl.BlockSpec(a.shape, lambda b, c: (0,) * a.ndim)
    return pl.pallas_call(
        _mlstm_kernel,
        grid=(batch, nc),
        in_specs=[row(2 * d), row(d), row(d), row(LANES),
                  pl.BlockSpec((None, 8, L), lambda b, c: (b, 0, c)),
                  full(conv_w), full(conv_b), full(gate_col), full(gate_row), full(norm_g)],
        out_specs=row(d),
        out_shape=jax.ShapeDtypeStruct((t, d), BF16),
        scratch_shapes=[pltpu.VMEM((L + 8, 2 * d), F32),
                        pltpu.VMEM((MLSTM_HEADS, MLSTM_HD, MLSTM_HD), F32),
                        pltpu.VMEM((MLSTM_HEADS, 1, MLSTM_HD), F32),
                        pltpu.VMEM((8, LANES), F32)],
        compiler_params=pltpu.CompilerParams(dimension_semantics=("arbitrary", "arbitrary"),
                                             vmem_limit_bytes=VMEM_LIMIT),
        name="mlstm",
    )(qk, v, og, ifc, ifr, conv_w, conv_b, gate_col, gate_row, norm_g)


def _proj_b_kernel(h_ref, w_ref, cosd_ref, sind_ref, gq_ref, gk_ref,
                   qt_ref, k_ref, vt_ref, z_ref, qit_ref, ki_ref, wit_ref):
    d = DSA_HEADS * DSA_HD
    h = h_ref[...]
    cosd = cosd_ref[...]
    sind = sind_ref[...]
    cosi, sini = cosd, sind

    acc = _dot(h, w_ref[:, 0:d])
    for hd in range(DSA_HEADS):
        sl = slice(hd * DSA_HD, (hd + 1) * DSA_HD)
        xs = _rotate(_rms_rows(acc[:, sl], gq_ref[...]), cosd, sind) * (DSA_HD ** -0.5 * LOG2E)
        qt_ref[sl, :] = xs.T.astype(BF16)
    acc = _dot(h, w_ref[:, d:2 * d])
    for hd in range(DSA_HEADS):
        sl = slice(hd * DSA_HD, (hd + 1) * DSA_HD)
        k_ref[:, sl] = _rotate(_rms_rows(acc[:, sl], gk_ref[...]), cosd, sind).astype(BF16)
    acc = _dot(h, w_ref[:, 2 * d:3 * d])
    for hd in range(DSA_HEADS):
        sl = slice(hd * DSA_HD, (hd + 1) * DSA_HD)
        vt_ref[sl, :] = acc[:, sl].T.astype(BF16)
    z_ref[...] = _silu(_dot(h, w_ref[:, 3 * d:4 * d]))
    acc = _dot(h, w_ref[:, 4 * d:4 * d + IDX_HEADS * LANES])
    for hd in range(IDX_HEADS):
        sl = slice(hd * LANES, (hd + 1) * LANES)
        qit_ref[sl, :] = (_rotate(acc[:, sl], cosi, sini) * (IDX_HD ** -0.5)).T.astype(BF16)
    c0 = 4 * d + IDX_HEADS * LANES
    ki_ref[...] = _rotate(_dot(h, w_ref[:, c0:c0 + LANES]), cosi, sini).astype(BF16)
    wi = _dot(h, w_ref[:, c0 + LANES:c0 + 2 * LANES]) * (IDX_HEADS ** -0.5)
    wit_ref[...] = wi.T[0:IDX_HEADS, :]


def _proj_b(hn, w_b, cosd, sind, gq, gk):
    t = hn.shape[0]
    d = DSA_HEADS * DSA_HD
    tm = DSA_TILE
    nt = t // tm
    row = lambda n: pl.BlockSpec((tm, n), lambda i: (i, 0))
    full = lambda a: pl.BlockSpec(a.shape, lambda i: (0, 0))
    fmaj = lambda n: pl.BlockSpec((None, n, tm), lambda i: (i, 0, 0))
    sds = jax.ShapeDtypeStruct
    return pl.pallas_call(
        _proj_b_kernel,
        grid=(nt,),
        in_specs=[row(D_MODEL), full(w_b), row(LANES), row(LANES), full(gq), full(gk)],
        out_specs=[fmaj(d), row(d), fmaj(d), row(d), fmaj(IDX_HEADS * LANES), row(LANES), fmaj(IDX_HEADS)],
        out_shape=[sds((nt, d, tm), BF16), sds((t, d), BF16), sds((nt, d, tm), BF16), sds((t, d), F32),
                   sds((nt, IDX_HEADS * LANES, tm), BF16), sds((t, LANES), BF16),
                   sds((nt, IDX_HEADS, tm), F32)],
        compiler_params=pltpu.CompilerParams(dimension_semantics=("arbitrary",),
                                             vmem_limit_bytes=VMEM_LIMIT),
        name="proj_b",
    )(hn, w_b, cosd, sind, gq, gk)


def _sortable_key(x):
    bits = pltpu.bitcast(x, I32)
    return bits ^ ((bits >> 31) & jnp.int32(0x7FFFFFFF))


def _fold_rows(x, rows, op=jnp.add):
    acc = x[0:rows, :]
    for r in range(1, x.shape[0] // rows):
        acc = op(acc, x[r * rows:(r + 1) * rows, :])
    return acc


def _reduce_rows8(x8, op):
    for shift in (4, 2, 1):
        x8 = op(x8, pltpu.roll(x8, shift, 0))
    return x8[0:1, :]


def _dsa_kernel(qt_ref, qit_ref, wit_ref, zb_ref, ki_ref, k_hbm, vt_hbm, o_ref,
                kbuf, vbuf, key_ref, k16_ref, gmax_ref, acc_ref, m_ref, s_ref, mt_ref, sem,
                *, top_k):
    T = DSA_TILE
    C = SCORE_ROWS
    b = pl.program_id(0)
    j = pl.program_id(1)

    def kv_copies():
        return (pltpu.make_async_copy(k_hbm.at[b], kbuf, sem.at[0]),
                pltpu.make_async_copy(vt_hbm.at[b], vbuf, sem.at[1]))

    @pl.when(j == 0)
    def _():
        for cp in kv_copies():
            cp.start()

    nt = j + 1
    t_q = j * T + lax.broadcasted_iota(I32, (1, T), 1)
    vis_end = (t_q // VIS_CHUNK + 1) * VIS_CHUNK
    row_pos = lax.broadcasted_iota(I32, (T, T), 0)
    row_pos_c = lax.broadcasted_iota(I32, (C, T), 0)

    gmax_ref[...] = jnp.full((T, T), INT_MIN, I32)

    def score_tile(t, diagonal):
        for c in range(T // C):
            rows = slice(c * C, (c + 1) * C)
            sc = jnp.zeros((C, T), F32)
            for h in range(IDX_HEADS):
                lg = _dot(ki_ref[t, rows, :], qit_ref[h * LANES:(h + 1) * LANES, :])
                sc = sc + jnp.maximum(lg, 0.0) * wit_ref[h:h + 1, :]
            key = _sortable_key(sc)
            if diagonal:
                key = jnp.where(t * T + c * C + row_pos_c < vis_end, key, INT_MIN)
            key_ref[t, rows, :] = key
            k16_ref[t, rows, :] = (key >> 16).astype(I16)
            gmax_ref[rows, :] = jnp.maximum(gmax_ref[rows, :], key)

    def score_body(pair, carry):
        score_tile(2 * pair, False)
        score_tile(2 * pair + 1, False)
        return carry

    lax.fori_loop(0, (nt - 1) // 2, score_body, 0)

    @pl.when(nt % 2 == 0)
    def _():
        score_tile(nt - 2, False)

    score_tile(nt - 1, True)

    @pl.when(nt % 2 == 1)
    def _():
        k16_ref[nt] = jnp.full((T, T), -32768, I16)

    def count16_ge(cand_row):
        cand = jnp.broadcast_to(cand_row.astype(I16), (PACKED_ROWS, T))
        n_acc = 4

        def body(pair, accs):
            accs = list(accs)
            for u in range(2):
                for r in range(T // PACKED_ROWS):
                    blk = k16_ref[2 * pair + u, r * PACKED_ROWS:(r + 1) * PACKED_ROWS, :]
                    accs[r % n_acc] = accs[r % n_acc] + jnp.where(blk >= cand, jnp.int16(1), jnp.int16(0))
            return tuple(accs)

        zero = jnp.zeros((PACKED_ROWS, T), I16)
        accs = lax.fori_loop(0, (nt + 1) // 2, body, (zero,) * n_acc)
        acc = (accs[0] + accs[1]) + (accs[2] + accs[3])
        return jnp.sum(acc.astype(I32), axis=0, keepdims=True)

    gm = gmax_ref[...]
    lo0 = _reduce_rows8(_fold_rows(gm, SUBLANES, jnp.minimum), jnp.minimum) >> 16
    hi0 = _reduce_rows8(_fold_rows(gm, SUBLANES, jnp.maximum), jnp.maximum) >> 16
    span = jnp.max((hi0 - lo0).astype(F32)).astype(I32)
    n_iter = jnp.int32(0)
    for bit in range(16):
        n_iter = n_iter + jnp.where((span >> bit) > 0, 1, 0)

    def hi_body(i, carry):
        lo, hi = carry
        mid = lo + ((hi - lo + 1) >> 1)
        ok = count16_ge(mid) >= top_k
        return jnp.where(ok, mid, lo), jnp.where(ok, hi, mid - 1)

    thr_hi, _ = lax.fori_loop(0, n_iter, hi_body, (lo0, hi0))
    n_above = count16_ge(thr_hi + 1) * jnp.where(thr_hi == 32767, 0, 1)

    def low_body(t, carry):
        key = key_ref[t]
        low = (key & 0xFFFF) - 32768
        k16_ref[t] = jnp.where((key >> 16) == thr_hi, low, -32768).astype(I16)
        return carry

    lax.fori_loop(0, nt, low_body, 0)
    need_lo = top_k - n_above

    def lo_body(i, thr_u):
        cand_u = thr_u | lax.shift_left(jnp.int32(1), 15 - i)
        return jnp.where(count16_ge(cand_u - 32768) >= need_lo, cand_u, thr_u)

    thr_lo = lax.fori_loop(0, 16, lo_body, jnp.zeros((1, T), I32)) - 32768
    thr = thr_hi * 65536 + (thr_lo + 32768)

    n_gt = n_above + count16_ge(thr_lo + 1) * jnp.where(thr_lo == 32767, 0, 1)
    n_ge = n_above + count16_ge(thr_lo)
    need = (top_k - n_gt).astype(F32)
    surplus = jnp.max(jnp.where(thr_lo == -32768, 1, n_ge - top_k).astype(F32))

    @pl.when(surplus > 0.0)
    def _():
        c_i = lax.broadcasted_iota(I32, (T, T), 1)
        prefix = jnp.where(c_i <= row_pos, 1.0, 0.0).astype(BF16)
        demoted = jnp.where(thr == INT_MIN, INT_MIN, thr - 1)

        def tie_body(t, seen):
            key = key_ref[t]
            eq = key == thr
            eqf = jnp.where(eq, 1.0, 0.0)
            incl = _dot(prefix, eqf.astype(BF16))
            rank = seen + incl - eqf
            key_ref[t] = jnp.where(eq & (rank >= need), demoted, key)
            return seen + incl[T - 1:T, :]

        lax.fori_loop(0, nt, tie_body, jnp.zeros((1, T), F32))

    def bias_tile(t, diagonal):
        keep = jnp.where(key_ref[t] >= thr, 0.0, NEG_BIG)
        if diagonal:
            keep = jnp.where(t * T + row_pos < vis_end, keep, NEG_BIG)
        key_ref[t] = pltpu.bitcast(keep.astype(F32), I32)

    def bias_body(t, carry):
        bias_tile(t, False)
        return carry

    lax.fori_loop(0, nt - 1, bias_body, 0)
    bias_tile(nt - 1, True)

    m_ref[...] = jnp.full(m_ref.shape, NEG_BIG, F32)
    acc_ref[...] = jnp.zeros_like(acc_ref)
    ones_rows = jnp.ones((PACKED_ROWS, T), BF16)

    @pl.when(j == 0)
    def _():
        for cp in kv_copies():
            cp.wait()

    def logits(t, h):
        sl = slice(h * DSA_HD, (h + 1) * DSA_HD)
        s = _dot(kbuf[t, :, sl], qt_ref[sl, :]) + pltpu.bitcast(key_ref[t], F32)
        s_ref[h] = s
        mt_ref[h] = _fold_rows(s, SUBLANES, jnp.maximum)

    def accumulate(t, h):
        sl = slice(h * DSA_HD, (h + 1) * DSA_HD)
        m_old = m_ref[h]
        m_new = jnp.maximum(m_old, jnp.max(mt_ref[h], axis=0, keepdims=True))
        alpha = jnp.exp2(m_old - m_new)
        p = jnp.exp2(s_ref[h] - m_new)
        v_aug = jnp.concatenate([vbuf[t, sl, :], ones_rows], axis=0)
        acc_ref[h] = alpha * acc_ref[h] + _dot(v_aug, p.astype(BF16))
        m_ref[h] = m_new

    for h in range(DSA_HEADS):
        logits(0, h)

    def attn_step(t):
        for h in range(DSA_HEADS):
            accumulate(t, h)
            logits(t + 1, h)

    def attn_body(pair, carry):
        attn_step(2 * pair)
        attn_step(2 * pair + 1)
        return carry

    lax.fori_loop(0, (nt - 1) // 2, attn_body, 0)

    @pl.when((nt - 1) % 2 == 1)
    def _():
        attn_step(nt - 2)

    for h in range(DSA_HEADS):
        accumulate(nt - 1, h)

    for h in range(DSA_HEADS):
        sl = slice(h * DSA_HD, (h + 1) * DSA_HD)
        out_t = acc_ref[h, 0:DSA_HD, :] / acc_ref[h, DSA_HD:DSA_HD + 1, :]
        o_ref[:, sl] = (out_t.T * zb_ref[:, sl]).astype(BF16)


def _dsa(qt, k, vt, zb, qit, ki, wit, batch):
    t, d = k.shape
    s = t // batch
    T = DSA_TILE
    nb = s // T
    top_k = min(DSA_TOPK, s // 4)
    assert top_k <= T and nb % 2 == 0
    k4 = k.reshape(batch, nb, T, d)
    vt4 = vt.reshape(batch, nb, d, T)
    ki4 = ki.reshape(batch, nb, T, LANES)
    fmaj = lambda n: pl.BlockSpec((None, n, T), lambda b, j: (b * nb + j, 0, 0))
    row = lambda n: pl.BlockSpec((T, n), lambda b, j: (b * nb + j, 0))
    return pl.pallas_call(
        functools.partial(_dsa_kernel, top_k=top_k),
        grid=(batch, nb),
        in_specs=[fmaj(d), fmaj(IDX_HEADS * LANES), fmaj(IDX_HEADS), row(d),
                  pl.BlockSpec((None, nb, T, LANES), lambda b, j: (b, 0, 0, 0), pipeline_mode=pl.Buffered(1)),
                  pl.BlockSpec(memory_space=pl.ANY), pl.BlockSpec(memory_space=pl.ANY)],
        out_specs=row(d),
        out_shape=jax.ShapeDtypeStruct((t, d), BF16),
        scratch_shapes=[pltpu.VMEM((nb, T, d), BF16), pltpu.VMEM((nb, d, T), BF16),
                        pltpu.VMEM((nb, T, T), I32), pltpu.VMEM((nb, T, T), I16),
                        pltpu.VMEM((T, T), I32),
                        pltpu.VMEM((DSA_HEADS, DSA_HD + PACKED_ROWS, T), F32),
                        pltpu.VMEM((DSA_HEADS, 1, T), F32),
                        pltpu.VMEM((DSA_HEADS, T, T), F32),
                        pltpu.VMEM((DSA_HEADS, SUBLANES, T), F32),
                        pltpu.SemaphoreType.DMA((2,))],
        compiler_params=pltpu.CompilerParams(dimension_semantics=("arbitrary", "arbitrary"),
                                             vmem_limit_bytes=VMEM_LIMIT),
        name="dsa",
    )(qt, qit, wit, zb, ki4, k4, vt4)


def _mem_kv_kernel(mem_ref, g_ref, w_ref, gk_ref, km_ref, vm_ref):
    d = MEM_HEADS * MEM_HD
    mh = _rms_rows(mem_ref[...], g_ref[...]).astype(BF16)
    kv = _dot(mh, w_ref[...])
    for h in range(MEM_HEADS):
        sl = slice(h * MEM_HD, (h + 1) * MEM_HD)
        km_ref[:, sl] = _rms_rows(kv[:, sl], gk_ref[...]).astype(BF16)
    vm_ref[...] = kv[:, d:2 * d].astype(BF16)


def _mem_kv(mem, g, w, gk):
    batch, nm, _ = mem.shape
    d = MEM_HEADS * MEM_HD
    full = lambda a: pl.BlockSpec(a.shape, lambda b: (0, 0))
    blk = lambda n: pl.BlockSpec((None, nm, n), lambda b: (b, 0, 0))
    return pl.pallas_call(
        _mem_kv_kernel,
        grid=(batch,),
        in_specs=[blk(D_MODEL), full(g), full(w), full(gk)],
        out_specs=[blk(d), blk(d)],
        out_shape=[jax.ShapeDtypeStruct((batch, nm, d), BF16)] * 2,
        compiler_params=pltpu.CompilerParams(dimension_semantics=("arbitrary",),
                                             vmem_limit_bytes=VMEM_LIMIT),
        name="mem_kv",
    )(mem, g, w, gk)


def _mem_attn_kernel(h_ref, w_ref, gq_ref, km_ref, vm_ref, y_ref):
    d = MEM_HEADS * MEM_HD
    h = h_ref[...]
    qm = _dot(h, w_ref[:, 0:d])
    zm = _dot(h, w_ref[:, d:2 * d])
    for hd in range(MEM_HEADS):
        sl = slice(hd * MEM_HD, (hd + 1) * MEM_HD)
        qs = (_rms_rows(qm[:, sl], gq_ref[...]) * (MEM_HD ** -0.5)).astype(BF16)
        s = _dot_nt(qs, km_ref[:, sl])
        p = jnp.exp(s - jnp.max(s, axis=-1, keepdims=True))
        o = _dot(p.astype(BF16), vm_ref[:, sl]) / jnp.sum(p, axis=-1, keepdims=True)
        y_ref[:, sl] = (o * _silu(zm[:, sl])).astype(BF16)


def _mem_attn(hn, w_m, gq, km, vm):
    t = hn.shape[0]
    batch, nm, d = km.shape
    tm = ROW_TILE
    per_b = t // batch // tm
    row = lambda n: pl.BlockSpec((tm, n), lambda i: (i, 0))
    full = lambda a: pl.BlockSpec(a.shape, lambda i: (0, 0))
    mem = pl.BlockSpec((None, nm, d), lambda i: (i // per_b, 0, 0))
    return pl.pallas_call(
        _mem_attn_kernel,
        grid=(t // tm,),
        in_specs=[row(D_MODEL), full(w_m), full(gq), mem, mem],
        out_specs=row(d),
        out_shape=jax.ShapeDtypeStruct((t, d), BF16),
        compiler_params=pltpu.CompilerParams(dimension_semantics=("arbitrary",),
                                             vmem_limit_bytes=VMEM_LIMIT),
        name="mem_attn",
    )(hn, w_m, gq, km, vm)


def _merge_kernel(x_ref, h_ref, wg_ref, ya_ref, yb_ref, ym_ref, wb_ref, wo_ref, o_ref):
    h = h_ref[...]
    merged = None
    for i, y_ref in enumerate((ya_ref, yb_ref, ym_ref)):
        gate = _sigmoid(_dot(h, wg_ref[:, i * D_MODEL:(i + 1) * D_MODEL]))
        term = gate * _dot(y_ref[...], wb_ref[i])
        merged = term if merged is None else merged + term
    o_ref[...] = x_ref[...] + _dot(merged.astype(BF16), wo_ref[...])


def _merge(x2, hn, w_gate, ya, yb, ym, w_branch, w_out):
    t = x2.shape[0]
    tm = ROW_TILE
    row = lambda n: pl.BlockSpec((tm, n), lambda i: (i, 0))
    full = lambda a: pl.BlockSpec(a.shape, lambda i: (0,) * a.ndim)
    return pl.pallas_call(
        _merge_kernel,
        grid=(t // tm,),
        in_specs=[row(D_MODEL), row(D_MODEL), full(w_gate), row(D_MODEL), row(D_MODEL), row(D_MODEL),
                  full(w_branch), full(w_out)],
        out_specs=row(D_MODEL),
        out_shape=jax.ShapeDtypeStruct((t, D_MODEL), F32),
        compiler_params=pltpu.CompilerParams(dimension_semantics=("arbitrary",),
                                             vmem_limit_bytes=VMEM_LIMIT),
        name="merge",
    )(x2, hn, w_gate, ya, yb, ym, w_branch, w_out)


def _rope_tables(positions):
    half = DSA_HD // 2
    inv_freq = jnp.power(ROPE_THETA, -jnp.arange(half, dtype=F32) / half)
    ang = positions.astype(F32)[..., None] * inv_freq
    cos, sin = jnp.cos(ang), jnp.sin(ang)
    cos_t = jnp.concatenate([cos, cos], axis=-1)
    sin_t = jnp.concatenate([-sin, sin], axis=-1)
    return cos_t.reshape(-1, LANES), sin_t.reshape(-1, LANES)


def _pad_idx_cols(w, heads):
    dm = w.shape[0]
    half = IDX_HD // 2
    w5 = w.reshape(dm, heads, 2, half, 1)
    w5 = jnp.concatenate([w5, jnp.zeros_like(w5)], axis=-1)
    return w5.reshape(dm, heads * LANES)


def _pad_cols(w, n):
    return jnp.pad(w, ((0, 0), (0, n - w.shape[1])))


def _layer(x, mem, positions, norm_g, mem_norm_g, w_in, conv_w, conv_b, gate_b, mlstm_norm_g,
           dsa_q_norm_g, dsa_k_norm_g, mem_q_norm_g, mem_k_norm_g, w_mem_kv, w_branch, w_out):
    batch, seq, dm = x.shape
    x2 = x.reshape(batch * seq, dm)
    d = D_MODEL
    splits = (2 * d, d, 2 * MLSTM_HEADS, d, d, d, d, d, d, IDX_HEADS * IDX_HD, IDX_HD, IDX_HEADS,
              d, d, N_BRANCH * d)
    cuts = np.cumsum(np.array(splits))[:-1].tolist()
    (w_qk, w_va, w_if, w_oa, w_za, w_qb, w_kb, w_vb, w_zb, w_qi, w_ki, w_wi,
     w_qm, w_zm, w_gate) = jnp.split(w_in, cuts, axis=-1)
    g = norm_g.reshape(1, dm)

    w_a = jnp.concatenate([w_qk, w_va, w_oa, w_za], axis=1).astype(BF16)
    qk, va, og, ifc, hn = _proj_a(x2, g, w_a, _pad_cols(w_if, LANES).astype(BF16))
    ifr = ifc[:, :8].reshape(batch, seq, 8).transpose(0, 2, 1)
    gb = gate_b.reshape(2 * MLSTM_HEADS)
    ya = _mlstm(qk, va, og, ifc, ifr, conv_w, conv_b.reshape(1, -1),
                _pad_cols(gb.reshape(1, -1), LANES), gb.reshape(-1, 1),
                mlstm_norm_g.reshape(MLSTM_HEADS, 1, MLSTM_HD), batch)

    w_b = jnp.concatenate([w_qb, w_kb, w_vb, w_zb, _pad_idx_cols(w_qi, IDX_HEADS),
                           _pad_idx_cols(w_ki, 1), _pad_cols(w_wi, LANES)], axis=1).astype(BF16)
    cosd, sind = _rope_tables(positions)
    qt, kb, vt, zb, qit, ki, wit = _proj_b(hn, w_b, cosd, sind,
                                           dsa_q_norm_g.reshape(1, -1), dsa_k_norm_g.reshape(1, -1))
    yb = _dsa(qt, kb, vt, zb, qit, ki, wit, batch)

    km, vm = _mem_kv(mem, mem_norm_g.reshape(1, dm), w_mem_kv.astype(BF16), mem_k_norm_g.reshape(1, -1))
    ym = _mem_attn(hn, jnp.concatenate([w_qm, w_zm], axis=1).astype(BF16),
                   mem_q_norm_g.reshape(1, -1), km, vm)

    out = _merge(x2, hn, w_gate.astype(BF16), ya, yb, ym, w_branch.astype(BF16), w_out.astype(BF16))
    return out.reshape(batch, seq, dm)


def kernel(x, mem, positions, norm_g, mem_norm_g, w_in, conv_w, conv_b, mlstm_gate_b, mlstm_norm_g,
           dsa_q_norm_g, dsa_k_norm_g, mem_q_norm_g, mem_k_norm_g, w_mem_kv, w_branch, w_out):
    for l in range(norm_g.shape[0]):
        x = _layer(x, mem, positions, norm_g[l], mem_norm_g[l], w_in[l], conv_w[l], conv_b[l],
                   mlstm_gate_b[l], mlstm_norm_g[l], dsa_q_norm_g[l], dsa_k_norm_g[l],
                   mem_q_norm_g[l], mem_k_norm_g[l], w_mem_kv[l], w_branch[l], w_out[l])
    return x
```

```python
import functools
import math

import numpy as np
import jax
import jax.numpy as jnp
from jax import lax
from jax.experimental import pallas as pl
from jax.experimental.pallas import tpu as pltpu

F32 = jnp.float32
BF16 = jnp.bfloat16
I16 = jnp.int16
I32 = jnp.int32

EPS = 1e-6
ROPE_THETA = 10000.0
D_MODEL = 1024
MLSTM_HEADS = 4
MLSTM_HD = 256
CONV_W = 4
DSA_HEADS = 8
DSA_HD = 128
IDX_HEADS = 8
IDX_HD = 64
DSA_TOPK = 256
VIS_CHUNK = 64
MEM_HEADS = 4
MEM_HD = 256
N_BRANCH = 3

LANES = 128
SUBLANES = 8
PACKED_ROWS = 16
SCAN_CHUNK = 256
DSA_TILE = 256
SCORE_ROWS = 64
ROW_TILE = 512
NEG_BIG = -1e30
INT_MIN = -2 ** 31
LOG2E = math.log2(math.e)
VMEM_LIMIT = 56 * 1024 * 1024


def _sigmoid(x):
    return 1.0 / (1.0 + jnp.exp(-x))


def _silu(x):
    return x * _sigmoid(x)


def _log_sigmoid(x):
    return jnp.minimum(x, 0.0) - jnp.log(1.0 + jnp.exp(-jnp.abs(x)))


def _dot(a, b):
    return jnp.dot(a, b, preferred_element_type=F32)


def _dot_nt(a, b):
    return lax.dot_general(a, b, (((1,), (1,)), ((), ())), preferred_element_type=F32)


def _dot_tn(a, b):
    return lax.dot_general(a, b, (((0,), (0,)), ((), ())), preferred_element_type=F32)


def _rms_rows(x, g):
    ms = jnp.mean(x * x, axis=-1, keepdims=True)
    return x * lax.rsqrt(ms + EPS) * g


def _normed_input(x_ref, g_ref):
    return _rms_rows(x_ref[...], g_ref[...]).astype(BF16)


def _rotate(x, cos, sin_signed):
    return x * cos + pltpu.roll(x, LANES // 2, 1) * sin_signed


def _proj_a_kernel(x_ref, g_ref, w_ref, wif_ref, qk_ref, v_ref, og_ref, if_ref, h_ref):
    d = MLSTM_HEADS * MLSTM_HD
    h = _normed_input(x_ref, g_ref)
    h_ref[...] = h
    qk_ref[...] = _dot(h, w_ref[:, 0:2 * d])
    v_ref[...] = _dot(h, w_ref[:, 2 * d:3 * d]).astype(BF16)
    o = _dot(h, w_ref[:, 3 * d:4 * d])
    z = _dot(h, w_ref[:, 4 * d:5 * d])
    og_ref[...] = _sigmoid(o) * _silu(z)
    if_ref[...] = _dot(h, wif_ref[...])


def _proj_a(x2, g, w_a, w_if):
    t = x2.shape[0]
    d = MLSTM_HEADS * MLSTM_HD
    tm = ROW_TILE
    row = lambda n: pl.BlockSpec((tm, n), lambda i: (i, 0))
    full = lambda a: pl.BlockSpec(a.shape, lambda i: (0, 0))
    return pl.pallas_call(
        _proj_a_kernel,
        grid=(t // tm,),
        in_specs=[row(D_MODEL), full(g), full(w_a), full(w_if)],
        out_specs=[row(2 * d), row(d), row(d), row(LANES), row(D_MODEL)],
        out_shape=[jax.ShapeDtypeStruct((t, 2 * d), F32), jax.ShapeDtypeStruct((t, d), BF16),
                   jax.ShapeDtypeStruct((t, d), F32), jax.ShapeDtypeStruct((t, LANES), F32),
                   jax.ShapeDtypeStruct((t, D_MODEL), BF16)],
        compiler_params=pltpu.CompilerParams(dimension_semantics=("arbitrary",),
                                             vmem_limit_bytes=VMEM_LIMIT),
        name="proj_a",
    )(x2, g, w_a, w_if)


def _mlstm_kernel(qk_ref, v_ref, og_ref, ifc_ref, ifr_ref, cw_ref, cb_ref, gbc_ref, gbr_ref, ng_ref,
                  y_ref, ext_ref, c_ref, n_ref, m_ref):
    L = SCAN_CHUNK
    dh = MLSTM_HD
    d = MLSTM_HEADS * dh
    halo = 8

    @pl.when(pl.program_id(1) == 0)
    def _():
        ext_ref[0:halo, :] = jnp.zeros((halo, 2 * d), F32)
        c_ref[...] = jnp.zeros_like(c_ref)
        n_ref[...] = jnp.zeros_like(n_ref)
        m_ref[...] = jnp.zeros_like(m_ref)

    ext_ref[halo:halo + L, :] = qk_ref[...]

    def conv_silu(c0):
        u = cb_ref[:, c0:c0 + dh]
        for j in range(CONV_W):
            r0 = halo - (CONV_W - 1) + j
            u = u + cw_ref[j:j + 1, c0:c0 + dh] * ext_ref[r0:r0 + L, c0:c0 + dh]
        return _silu(u)

    row_i = lax.broadcasted_iota(I32, (L, L), 0)
    col_i = lax.broadcasted_iota(I32, (L, L), 1)
    causal = row_i >= col_i
    tril = jnp.where(causal, 1.0, 0.0).astype(F32)
    triu = jnp.where(col_i >= row_i, 1.0, 0.0).astype(F32)

    ifc = ifc_ref[...] + gbc_ref[...]
    ifr = ifr_ref[...] + gbr_ref[...]
    bcol_all = jnp.dot(tril, _log_sigmoid(ifc), preferred_element_type=F32,
                       precision=lax.Precision.HIGHEST)
    brow_all = jnp.dot(_log_sigmoid(ifr), triu, preferred_element_type=F32,
                       precision=lax.Precision.HIGHEST)

    for h in range(MLSTM_HEADS):
        qf = conv_silu(h * dh)
        kf = conv_silu(d + h * dh) * (dh ** -0.5)
        q = qf.astype(BF16)
        v = v_ref[:, h * dh:(h + 1) * dh]
        icol = ifc[:, h:h + 1]
        irow = ifr[h:h + 1, :]
        bcol = bcol_all[:, MLSTM_HEADS + h:MLSTM_HEADS + h + 1]
        brow = brow_all[MLSTM_HEADS + h:MLSTM_HEADS + h + 1, :]
        m_prev = m_ref[h:h + 1, 0:1]

        d_log = jnp.where(causal, bcol - brow + irow, -jnp.inf)
        inter = bcol + m_prev
        m_t = jnp.maximum(inter, jnp.max(d_log, axis=-1, keepdims=True))
        w_intra = jnp.exp(d_log - m_t)
        w_inter = jnp.exp(inter - m_t)
        a = _dot_nt(q, kf.astype(BF16)) * w_intra
        nrow = n_ref[h]
        num = w_inter * _dot(q, c_ref[h].astype(BF16)) + _dot(a.astype(BF16), v)
        den = w_inter * jnp.sum(qf * nrow, axis=-1, keepdims=True) + jnp.sum(a, axis=-1, keepdims=True)
        hh = num / jnp.maximum(jnp.abs(den), jnp.exp(-m_t))
        hn = _rms_rows(hh, ng_ref[h])
        y_ref[:, h * dh:(h + 1) * dh] = (hn * og_ref[:, h * dh:(h + 1) * dh]).astype(BF16)

        b_last = bcol[L - 1:L, :]
        g_col = b_last - bcol + icol
        m_new = jnp.maximum(b_last + m_prev, jnp.max(g_col, axis=0, keepdims=True))
        decay = jnp.exp(b_last + m_prev - m_new)
        kw = kf * jnp.exp(g_col - m_new)
        c_ref[h] = decay * c_ref[h] + _dot(kw.T.astype(BF16), v)
        n_ref[h] = decay * nrow + jnp.sum(kw, axis=0, keepdims=True)
        m_ref[h:h + 1, :] = jnp.broadcast_to(m_new, (1, LANES))

    ext_ref[0:halo, :] = ext_ref[L:L + halo, :]


def _mlstm(qk, v, og, ifc, ifr, conv_w, conv_b, gate_col, gate_row, norm_g, batch):
    t = qk.shape[0]
    L = SCAN_CHUNK
    d = MLSTM_HEADS * MLSTM_HD
    nc = t // batch // L
    row = lambda n: pl.BlockSpec((L, n), lambda b, c: (b * nc + c, 0))
    full = lambda a: pl.BlockSpec(a.shape, lambda b, c: (0,) * a.ndim)
    return pl.pallas_call(
        _mlstm_kernel,
        grid=(batch, nc),
        in_specs=[row(2 * d), row(d), row(d), row(LANES),
                  pl.BlockSpec((None, 8, L), lambda b, c: (b, 0, c)),
                  full(conv_w), full(conv_b), full(gate_col), full(gate_row), full(norm_g)],
        out_specs=row(d),
        out_shape=jax.ShapeDtypeStruct((t, d), BF16),
        scratch_shapes=[pltpu.VMEM((L + 8, 2 * d), F32),
                        pltpu.VMEM((MLSTM_HEADS, MLSTM_HD, MLSTM_HD), F32),
                        pltpu.VMEM((MLSTM_HEADS, 1, MLSTM_HD), F32),
                        pltpu.VMEM((8, LANES), F32)],
        compiler_params=pltpu.CompilerParams(dimension_semantics=("arbitrary", "arbitrary"),
                                             vmem_limit_bytes=VMEM_LIMIT),
        name="mlstm",
    )(qk, v, og, ifc, ifr, conv_w, conv_b, gate_col, gate_row, norm_g)


def _proj_b_kernel(h_ref, w_ref, cosd_ref, sind_ref, gq_ref, gk_ref,
                   qt_ref, k_ref, vt_ref, z_ref, qit_ref, ki_ref, wit_ref):
    d = DSA_HEADS * DSA_HD
    h = h_ref[...]
    cosd = cosd_ref[...]
    sind = sind_ref[...]
    cosi, sini = cosd, sind

    acc = _dot(h, w_ref[:, 0:d])
    for hd in range(DSA_HEADS):
        sl = slice(hd * DSA_HD, (hd + 1) * DSA_HD)
        xs = _rotate(_rms_rows(acc[:, sl], gq_ref[...]), cosd, sind) * (DSA_HD ** -0.5 * LOG2E)
        qt_ref[sl, :] = xs.T.astype(BF16)
    acc = _dot(h, w_ref[:, d:2 * d])
    for hd in range(DSA_HEADS):
        sl = slice(hd * DSA_HD, (hd + 1) * DSA_HD)
        k_ref[:, sl] = _rotate(_rms_rows(acc[:, sl], gk_ref[...]), cosd, sind).astype(BF16)
    acc = _dot(h, w_ref[:, 2 * d:3 * d])
    for hd in range(DSA_HEADS):
        sl = slice(hd * DSA_HD, (hd + 1) * DSA_HD)
        vt_ref[sl, :] = acc[:, sl].T.astype(BF16)
    z_ref[...] = _silu(_dot(h, w_ref[:, 3 * d:4 * d]))
    acc = _dot(h, w_ref[:, 4 * d:4 * d + IDX_HEADS * LANES])
    for hd in range(IDX_HEADS):
        sl = slice(hd * LANES, (hd + 1) * LANES)
        qit_ref[sl, :] = (_rotate(acc[:, sl], cosi, sini) * (IDX_HD ** -0.5)).T.astype(BF16)
    c0 = 4 * d + IDX_HEADS * LANES
    ki_ref[...] = _rotate(_dot(h, w_ref[:, c0:c0 + LANES]), cosi, sini).astype(BF16)
    wi = _dot(h, w_ref[:, c0 + LANES:c0 + 2 * LANES]) * (IDX_HEADS ** -0.5)
    wit_ref[...] = wi.T[0:IDX_HEADS, :]


def _proj_b(hn, w_b, cosd, sind, gq, gk):
    t = hn.shape[0]
    d = DSA_HEADS * DSA_HD
    tm = DSA_TILE
    nt = t // tm
    row = lambda n: pl.BlockSpec((tm, n), lambda i: (i, 0))
    full = lambda a: pl.BlockSpec(a.shape, lambda i: (0, 0))
    fmaj = lambda n: pl.BlockSpec((None, n, tm), lambda i: (i, 0, 0))
    sds = jax.ShapeDtypeStruct
    return pl.pallas_call(
        _proj_b_kernel,
        grid=(nt,),
        in_specs=[row(D_MODEL), full(w_b), row(LANES), row(LANES), full(gq), full(gk)],
        out_specs=[fmaj(d), row(d), fmaj(d), row(d), fmaj(IDX_HEADS * LANES), row(LANES), fmaj(IDX_HEADS)],
        out_shape=[sds((nt, d, tm), BF16), sds((t, d), BF16), sds((nt, d, tm), BF16), sds((t, d), F32),
                   sds((nt, IDX_HEADS * LANES, tm), BF16), sds((t, LANES), BF16),
                   sds((nt, IDX_HEADS, tm), F32)],
        compiler_params=pltpu.CompilerParams(dimension_semantics=("arbitrary",),
                                             vmem_limit_bytes=VMEM_LIMIT),
        name="proj_b",
    )(hn, w_b, cosd, sind, gq, gk)


def _sortable_key(x):
    bits = pltpu.bitcast(x, I32)
    return bits ^ ((bits >> 31) & jnp.int32(0x7FFFFFFF))


def _fold_rows(x, rows, op=jnp.add):
    acc = x[0:rows, :]
    for r in range(1, x.shape[0] // rows):
        acc = op(acc, x[r * rows:(r + 1) * rows, :])
    return acc


def _reduce_rows8(x8, op):
    for shift in (4, 2, 1):
        x8 = op(x8, pltpu.roll(x8, shift, 0))
    return x8[0:1, :]


def _dsa_kernel(qt_ref, qit_ref, wit_ref, zb_ref, ki_ref, k_hbm, vt_hbm, o_ref,
                kbuf, vbuf, key_ref, k16_ref, gmax_ref, acc_ref, m_ref, s_ref, mt_ref, sem,
                *, top_k):
    T = DSA_TILE
    C = SCORE_ROWS
    b = pl.program_id(0)
    j = pl.program_id(1)

    def kv_copies():
        return (pltpu.make_async_copy(k_hbm.at[b], kbuf, sem.at[0]),
                pltpu.make_async_copy(vt_hbm.at[b], vbuf, sem.at[1]))

    @pl.when(j == 0)
    def _():
        for cp in kv_copies():
            cp.start()

    nt = j + 1
    t_q = j * T + lax.broadcasted_iota(I32, (1, T), 1)
    vis_end = (t_q // VIS_CHUNK + 1) * VIS_CHUNK
    row_pos = lax.broadcasted_iota(I32, (T, T), 0)
    row_pos_c = lax.broadcasted_iota(I32, (C, T), 0)

    gmax_ref[...] = jnp.full((T, T), INT_MIN, I32)

    def score_tile(t, diagonal):
        for c in range(T // C):
            rows = slice(c * C, (c + 1) * C)
            sc = jnp.zeros((C, T), F32)
            for h in range(IDX_HEADS):
                lg = _dot(ki_ref[t, rows, :], qit_ref[h * LANES:(h + 1) * LANES, :])
                sc = sc + jnp.maximum(lg, 0.0) * wit_ref[h:h + 1, :]
            key = _sortable_key(sc)
            if diagonal:
                key = jnp.where(t * T + c * C + row_pos_c < vis_end, key, INT_MIN)
            key_ref[t, rows, :] = key
            k16_ref[t, rows, :] = (key >> 16).astype(I16)
            gmax_ref[rows, :] = jnp.maximum(gmax_ref[rows, :], key)

    def score_body(pair, carry):
        score_tile(2 * pair, False)
        score_tile(2 * pair + 1, False)
        return carry

    lax.fori_loop(0, (nt - 1) // 2, score_body, 0)

    @pl.when(nt % 2 == 0)
    def _():
        score_tile(nt - 2, False)

    score_tile(nt - 1, True)

    @pl.when(nt % 2 == 1)
    def _():
        k16_ref[nt] = jnp.full((T, T), -32768, I16)

    def count16_ge(cand_row):
        cand = jnp.broadcast_to(cand_row.astype(I16), (PACKED_ROWS, T))
        n_acc = 4

        def body(pair, accs):
            accs = list(accs)
            for u in range(2):
                for r in range(T // PACKED_ROWS):
                    blk = k16_ref[2 * pair + u, r * PACKED_ROWS:(r + 1) * PACKED_ROWS, :]
                    accs[r % n_acc] = accs[r % n_acc] + jnp.where(blk >= cand, jnp.int16(1), jnp.int16(0))
            return tuple(accs)

        zero = jnp.zeros((PACKED_ROWS, T), I16)
        accs = lax.fori_loop(0, (nt + 1) // 2, body, (zero,) * n_acc)
        acc = (accs[0] + accs[1]) + (accs[2] + accs[3])
        return jnp.sum(acc.astype(I32), axis=0, keepdims=True)

    gm = gmax_ref[...]
    lo0 = _reduce_rows8(_fold_rows(gm, SUBLANES, jnp.minimum), jnp.minimum) >> 16
    hi0 = _reduce_rows8(_fold_rows(gm, SUBLANES, jnp.maximum), jnp.maximum) >> 16
    span = jnp.max((hi0 - lo0).astype(F32)).astype(I32)
    n_iter = jnp.int32(0)
    for bit in range(16):
        n_iter = n_iter + jnp.where((span >> bit) > 0, 1, 0)

    def hi_body(i, carry):
        lo, hi = carry
        mid = lo + ((hi - lo + 1) >> 1)
        ok = count16_ge(mid) >= top_k
        return jnp.where(ok, mid, lo), jnp.where(ok, hi, mid - 1)

    thr_hi, _ = lax.fori_loop(0, n_iter, hi_body, (lo0, hi0))
    n_above = count16_ge(thr_hi + 1) * jnp.where(thr_hi == 32767, 0, 1)

    def low_body(t, carry):
        key = key_ref[t]
        low = (key & 0xFFFF) - 32768
        k16_ref[t] = jnp.where((key >> 16) == thr_hi, low, -32768).astype(I16)
        return carry

    lax.fori_loop(0, nt, low_body, 0)
    need_lo = top_k - n_above

    def lo_body(i, thr_u):
        cand_u = thr_u | lax.shift_left(jnp.int32(1), 15 - i)
        return jnp.where(count16_ge(cand_u - 32768) >= need_lo, cand_u, thr_u)

    thr_lo = lax.fori_loop(0, 16, lo_body, jnp.zeros((1, T), I32)) - 32768
    thr = thr_hi * 65536 + (thr_lo + 32768)

    n_gt = n_above + count16_ge(thr_lo + 1) * jnp.where(thr_lo == 32767, 0, 1)
    n_ge = n_above + count16_ge(thr_lo)
    need = (top_k - n_gt).astype(F32)
    surplus = jnp.max(jnp.where(thr_lo == -32768, 1, n_ge - top_k).astype(F32))

    @pl.when(surplus > 0.0)
    def _():
        c_i = lax.broadcasted_iota(I32, (T, T), 1)
        prefix = jnp.where(c_i <= row_pos, 1.0, 0.0).astype(BF16)
        demoted = jnp.where(thr == INT_MIN, INT_MIN, thr - 1)

        def tie_body(t, seen):
            key = key_ref[t]
            eq = key == thr
            eqf = jnp.where(eq, 1.0, 0.0)
            incl = _dot(prefix, eqf.astype(BF16))
            rank = seen + incl - eqf
            key_ref[t] = jnp.where(eq & (rank >= need), demoted, key)
            return seen + incl[T - 1:T, :]

        lax.fori_loop(0, nt, tie_body, jnp.zeros((1, T), F32))

    def bias_tile(t, diagonal):
        keep = jnp.where(key_ref[t] >= thr, 0.0, NEG_BIG)
        if diagonal:
            keep = jnp.where(t * T + row_pos < vis_end, keep, NEG_BIG)
        key_ref[t] = pltpu.bitcast(keep.astype(F32), I32)

    def bias_body(t, carry):
        bias_tile(t, False)
        return carry

    lax.fori_loop(0, nt - 1, bias_body, 0)
    bias_tile(nt - 1, True)

    m_ref[...] = jnp.full(m_ref.shape, NEG_BIG, F32)
    acc_ref[...] = jnp.zeros_like(acc_ref)
    ones_rows = jnp.ones((PACKED_ROWS, T), BF16)

    @pl.when(j == 0)
    def _():
        for cp in kv_copies():
            cp.wait()

    def logits(t, h):
        sl = slice(h * DSA_HD, (h + 1) * DSA_HD)
        s = _dot(kbuf[t, :, sl], qt_ref[sl, :]) + pltpu.bitcast(key_ref[t], F32)
        s_ref[h] = s
        mt_ref[h] = _fold_rows(s, SUBLANES, jnp.maximum)

    def accumulate(t, h):
        sl = slice(h * DSA_HD, (h + 1) * DSA_HD)
        m_old = m_ref[h]
        m_new = jnp.maximum(m_old, jnp.max(mt_ref[h], axis=0, keepdims=True))
        alpha = jnp.exp2(m_old - m_new)
        p = jnp.exp2(s_ref[h] - m_new)
        v_aug = jnp.concatenate([vbuf[t, sl, :], ones_rows], axis=0)
        acc_ref[h] = alpha * acc_ref[h] + _dot(v_aug, p.astype(BF16))
        m_ref[h] = m_new

    for h in range(DSA_HEADS):
        logits(0, h)

    def attn_step(t):
        for h in range(DSA_HEADS):
            accumulate(t, h)
            logits(t + 1, h)

    def attn_body(pair, carry):
        attn_step(2 * pair)
        attn_step(2 * pair + 1)
        return carry

    lax.fori_loop(0, (nt - 1) // 2, attn_body, 0)

    @pl.when((nt - 1) % 2 == 1)
    def _():
        attn_step(nt - 2)

    for h in range(DSA_HEADS):
        accumulate(nt - 1, h)

    for h in range(DSA_HEADS):
        sl = slice(h * DSA_HD, (h + 1) * DSA_HD)
        out_t = acc_ref[h, 0:DSA_HD, :] / acc_ref[h, DSA_HD:DSA_HD + 1, :]
        o_ref[:, sl] = (out_t.T * zb_ref[:, sl]).astype(BF16)


def _dsa(qt, k, vt, zb, qit, ki, wit, batch):
    t, d = k.shape
    s = t // batch
    T = DSA_TILE
    nb = s // T
    top_k = min(DSA_TOPK, s // 4)
    assert top_k <= T and nb % 2 == 0
    k4 = k.reshape(batch, nb, T, d)
    vt4 = vt.reshape(batch, nb, d, T)
    ki4 = ki.reshape(batch, nb, T, LANES)
    fmaj = lambda n: pl.BlockSpec((None, n, T), lambda b, j: (b * nb + j, 0, 0))
    row = lambda n: pl.BlockSpec((T, n), lambda b, j: (b * nb + j, 0))
    return pl.pallas_call(
        functools.partial(_dsa_kernel, top_k=top_k),
        grid=(batch, nb),
        in_specs=[fmaj(d), fmaj(IDX_HEADS * LANES), fmaj(IDX_HEADS), row(d),
                  pl.BlockSpec((None, nb, T, LANES), lambda b, j: (b, 0, 0, 0), pipeline_mode=pl.Buffered(1)),
                  pl.BlockSpec(memory_space=pl.ANY), pl.BlockSpec(memory_space=pl.ANY)],
        out_specs=row(d),
        out_shape=jax.ShapeDtypeStruct((t, d), BF16),
        scratch_shapes=[pltpu.VMEM((nb, T, d), BF16), pltpu.VMEM((nb, d, T), BF16),
                        pltpu.VMEM((nb, T, T), I32), pltpu.VMEM((nb, T, T), I16),
                        pltpu.VMEM((T, T), I32),
                        pltpu.VMEM((DSA_HEADS, DSA_HD + PACKED_ROWS, T), F32),
                        pltpu.VMEM((DSA_HEADS, 1, T), F32),
                        pltpu.VMEM((DSA_HEADS, T, T), F32),
                        pltpu.VMEM((DSA_HEADS, SUBLANES, T), F32),
                        pltpu.SemaphoreType.DMA((2,))],
        compiler_params=pltpu.CompilerParams(dimension_semantics=("arbitrary", "arbitrary"),
                                             vmem_limit_bytes=VMEM_LIMIT),
        name="dsa",
    )(qt, qit, wit, zb, ki4, k4, vt4)


def _mem_kv_kernel(mem_ref, g_ref, w_ref, gk_ref, km_ref, vm_ref):
    d = MEM_HEADS * MEM_HD
    mh = _rms_rows(mem_ref[...], g_ref[...]).astype(BF16)
    kv = _dot(mh, w_ref[...])
    for h in range(MEM_HEADS):
        sl = slice(h * MEM_HD, (h + 1) * MEM_HD)
        km_ref[:, sl] = _rms_rows(kv[:, sl], gk_ref[...]).astype(BF16)
    vm_ref[...] = kv[:, d:2 * d].astype(BF16)


def _mem_kv(mem, g, w, gk):
    batch, nm, _ = mem.shape
    d = MEM_HEADS * MEM_HD
    full = lambda a: pl.BlockSpec(a.shape, lambda b: (0, 0))
    blk = lambda n: pl.BlockSpec((None, nm, n), lambda b: (b, 0, 0))
    return pl.pallas_call(
        _mem_kv_kernel,
        grid=(batch,),
        in_specs=[blk(D_MODEL), full(g), full(w), full(gk)],
        out_specs=[blk(d), blk(d)],
        out_shape=[jax.ShapeDtypeStruct((batch, nm, d), BF16)] * 2,
        compiler_params=pltpu.CompilerParams(dimension_semantics=("arbitrary",),
                                             vmem_limit_bytes=VMEM_LIMIT),
        name="mem_kv",
    )(mem, g, w, gk)


def _mem_attn_kernel(h_ref, w_ref, gq_ref, km_ref, vm_ref, y_ref):
    d = MEM_HEADS * MEM_HD
    h = h_ref[...]
    qm = _dot(h, w_ref[:, 0:d])
    zm = _dot(h, w_ref[:, d:2 * d])
    for hd in range(MEM_HEADS):
        sl = slice(hd * MEM_HD, (hd + 1) * MEM_HD)
        qs = (_rms_rows(qm[:, sl], gq_ref[...]) * (MEM_HD ** -0.5)).astype(BF16)
        s = _dot_nt(qs, km_ref[:, sl])
        p = jnp.exp(s - jnp.max(s, axis=-1, keepdims=True))
        o = _dot(p.astype(BF16), vm_ref[:, sl]) / jnp.sum(p, axis=-1, keepdims=True)
        y_ref[:, sl] = (o * _silu(zm[:, sl])).astype(BF16)


def _mem_attn(hn, w_m, gq, km, vm):
    t = hn.shape[0]
    batch, nm, d = km.shape
    tm = ROW_TILE
    per_b = t // batch // tm
    row = lambda n: pl.BlockSpec((tm, n), lambda i: (i, 0))
    full = lambda a: pl.BlockSpec(a.shape, lambda i: (0, 0))
    mem = pl.BlockSpec((None, nm, d), lambda i: (i // per_b, 0, 0))
    return pl.pallas_call(
        _mem_attn_kernel,
        grid=(t // tm,),
        in_specs=[row(D_MODEL), full(w_m), full(gq), mem, mem],
        out_specs=row(d),
        out_shape=jax.ShapeDtypeStruct((t, d), BF16),
        compiler_params=pltpu.CompilerParams(dimension_semantics=("arbitrary",),
                                             vmem_limit_bytes=VMEM_LIMIT),
        name="mem_attn",
    )(hn, w_m, gq, km, vm)


def _merge_kernel(x_ref, h_ref, wg_ref, ya_ref, yb_ref, ym_ref, wb_ref, wo_ref, o_ref):
    h = h_ref[...]
    merged = None
    for i, y_ref in enumerate((ya_ref, yb_ref, ym_ref)):
        gate = _sigmoid(_dot(h, wg_ref[:, i * D_MODEL:(i + 1) * D_MODEL]))
        term = gate * _dot(y_ref[...], wb_ref[i])
        merged = term if merged is None else merged + term
    o_ref[...] = x_ref[...] + _dot(merged.astype(BF16), wo_ref[...])


def _merge(x2, hn, w_gate, ya, yb, ym, w_branch, w_out):
    t = x2.shape[0]
    tm = ROW_TILE
    row = lambda n: pl.BlockSpec((tm, n), lambda i: (i, 0))
    full = lambda a: pl.BlockSpec(a.shape, lambda i: (0,) * a.ndim)
    return pl.pallas_call(
        _merge_kernel,
        grid=(t // tm,),
        in_specs=[row(D_MODEL), row(D_MODEL), full(w_gate), row(D_MODEL), row(D_MODEL), row(D_MODEL),
                  full(w_branch), full(w_out)],
        out_specs=row(D_MODEL),
        out_shape=jax.ShapeDtypeStruct((t, D_MODEL), F32),
        compiler_params=pltpu.CompilerParams(dimension_semantics=("arbitrary",),
                                             vmem_limit_bytes=VMEM_LIMIT),
        name="merge",
    )(x2, hn, w_gate, ya, yb, ym, w_branch, w_out)


def _rope_tables(positions):
    half = DSA_HD // 2
    inv_freq = jnp.power(ROPE_THETA, -jnp.arange(half, dtype=F32) / half)
    ang = positions.astype(F32)[..., None] * inv_freq
    cos, sin = jnp.cos(ang), jnp.sin(ang)
    cos_t = jnp.concatenate([cos, cos], axis=-1)
    sin_t = jnp.concatenate([-sin, sin], axis=-1)
    return cos_t.reshape(-1, LANES), sin_t.reshape(-1, LANES)


def _pad_idx_cols(w, heads):
    dm = w.shape[0]
    half = IDX_HD // 2
    w5 = w.reshape(dm, heads, 2, half, 1)
    w5 = jnp.concatenate([w5, jnp.zeros_like(w5)], axis=-1)
    return w5.reshape(dm, heads * LANES)


def _pad_cols(w, n):
    return jnp.pad(w, ((0, 0), (0, n - w.shape[1])))


def _layer(x, mem, positions, norm_g, mem_norm_g, w_in, conv_w, conv_b, gate_b, mlstm_norm_g,
           dsa_q_norm_g, dsa_k_norm_g, mem_q_norm_g, mem_k_norm_g, w_mem_kv, w_branch, w_out):
    batch, seq, dm = x.shape
    x2 = x.reshape(batch * seq, dm)
    d = D_MODEL
    splits = (2 * d, d, 2 * MLSTM_HEADS, d, d, d, d, d, d, IDX_HEADS * IDX_HD, IDX_HD, IDX_HEADS,
              d, d, N_BRANCH * d)
    cuts = np.cumsum(np.array(splits))[:-1].tolist()
    (w_qk, w_va, w_if, w_oa, w_za, w_qb, w_kb, w_vb, w_zb, w_qi, w_ki, w_wi,
     w_qm, w_zm, w_gate) = jnp.split(w_in, cuts, axis=-1)
    g = norm_g.reshape(1, dm)

    w_a = jnp.concatenate([w_qk, w_va, w_oa, w_za], axis=1).astype(BF16)
    qk, va, og, ifc, hn = _proj_a(x2, g, w_a, _pad_cols(w_if, LANES).astype(BF16))
    ifr = ifc[:, :8].reshape(batch, seq, 8).transpose(0, 2, 1)
    gb = gate_b.reshape(2 * MLSTM_HEADS)
    ya = _mlstm(qk, va, og, ifc, ifr, conv_w, conv_b.reshape(1, -1),
                _pad_cols(gb.reshape(1, -1), LANES), gb.reshape(-1, 1),
                mlstm_norm_g.reshape(MLSTM_HEADS, 1, MLSTM_HD), batch)

    w_b = jnp.concatenate([w_qb, w_kb, w_vb, w_zb, _pad_idx_cols(w_qi, IDX_HEADS),
                           _pad_idx_cols(w_ki, 1), _pad_cols(w_wi, LANES)], axis=1).astype(BF16)
    cosd, sind = _rope_tables(positions)
    qt, kb, vt, zb, qit, ki, wit = _proj_b(hn, w_b, cosd, sind,
                                           dsa_q_norm_g.reshape(1, -1), dsa_k_norm_g.reshape(1, -1))
    yb = _dsa(qt, kb, vt, zb, qit, ki, wit, batch)

    km, vm = _mem_kv(mem, mem_norm_g.reshape(1, dm), w_mem_kv.astype(BF16), mem_k_norm_g.reshape(1, -1))
    ym = _mem_attn(hn, jnp.concatenate([w_qm, w_zm], axis=1).astype(BF16),
                   mem_q_norm_g.reshape(1, -1), km, vm)

    out = _merge(x2, hn, w_gate.astype(BF16), ya, yb, ym, w_branch.astype(BF16), w_out.astype(BF16))
    return out.reshape(batch, seq, dm)


def kernel(x, mem, positions, norm_g, mem_norm_g, w_in, conv_w, conv_b, mlstm_gate_b, mlstm_norm_g,
           dsa_q_norm_g, dsa_k_norm_g, mem_q_norm_g, mem_k_norm_g, w_mem_kv, w_branch, w_out):
    for l in range(norm_g.shape[0]):
        x = _layer(x, mem, positions, norm_g[l], mem_norm_g[l], w_in[l], conv_w[l], conv_b[l],
                   mlstm_gate_b[l], mlstm_norm_g[l], dsa_q_norm_g[l], dsa_k_norm_g[l],
                   mem_q_norm_g[l], mem_k_norm_g[l], w_mem_kv[l], w_branch[l], w_out[l])
    return x
```
